```python
import math
import jax, jax.numpy as jnp
from jax import lax
import numpy as np

D_MODEL = 1024
BATCH = 4
SEQ = 8192
DEPTH = 2
DEC_BATCH = 32
DEC_SEQ = 4
PAST_LEN = 16384
PAGE_SIZE = 128

N_SSM_LAYERS = (DEPTH + 1) // 2
N_NSA_LAYERS = DEPTH // 2
NORM_EPS = 1e-6
SSM_CH = 16
SSM_GROUPS = D_MODEL // SSM_CH
SSM_STATE = 64
SSM_CHUNK = 128
DT_MIN = 1e-3
DT_MAX = 1e-1
N_HEADS = 16
HEAD_DIM = D_MODEL // N_HEADS
N_KV_HEADS = 4
HEADS_PER_GROUP = N_HEADS // N_KV_HEADS
KV_DIM = N_KV_HEADS * HEAD_DIM
L_CMP = 32
L_SEL = 64
CMP_PER_SEL = L_SEL // L_CMP
TOP_N = 16
WINDOW = 512
Q_BLOCK = 128
ROPE_THETA = 10000.0
NSA_PROJ = D_MODEL + 6 * KV_DIM + 3 * N_HEADS
NEG_INF = -1e30
FORCE_SCORE = 1e4
D_FF = 2816
CONV_W = 3

kernel_name = "hybrid_s5_nsa_convffn_step"


def rmsnorm(x, g):
    xf = x.astype(jnp.float32)
    y = xf * lax.rsqrt(jnp.mean(xf * xf, axis=-1, keepdims=True) + NORM_EPS)
    return (y * g.astype(jnp.float32)).astype(x.dtype)


def rope(x, pos):
    half = HEAD_DIM // 2
    inv = jnp.power(ROPE_THETA, -jnp.arange(half, dtype=jnp.float32) / half)
    ang = pos.astype(jnp.float32)[:, None] * inv[None, :]
    cos = jnp.cos(ang)[None, :, None, :]
    sin = jnp.sin(ang)[None, :, None, :]
    xf = x.astype(jnp.float32)
    x1, x2 = xf[..., :half], xf[..., half:]
    return jnp.concatenate([x1 * cos - x2 * sin, x2 * cos + x1 * sin], axis=-1).astype(x.dtype)


def masked_softmax(s, mask):
    s = jnp.where(mask, s.astype(jnp.float32), NEG_INF)
    return jnp.where(mask, jax.nn.softmax(s, axis=-1), 0.0)


def ssm_combine(e1, e2):
    a1r, a1i, b1r, b1i = e1
    a2r, a2i, b2r, b2i = e2
    return (a2r * a1r - a2i * a1i, a2r * a1i + a2i * a1r,
            a2r * b1r - a2i * b1i + b2r, a2r * b1i + a2i * b1r + b2i)


def ssm_mixer(xn, h_re, h_im, w_in, a_re, a_im, log_dt, b_re, b_im, c_re, c_im, d, w_gate, w_out):
    Bn, T, _ = xn.shape
    f32 = jnp.float32
    u = (xn @ w_in).astype(f32)
    ug = u.reshape(Bn, T, SSM_GROUPS, SSM_CH)
    a_re, a_im = a_re.astype(f32), a_im.astype(f32)
    dt = jnp.exp(log_dt.astype(f32))[:, None]
    mag = jnp.exp(a_re * dt)
    lam_re, lam_im = mag * jnp.cos(a_im * dt), mag * jnp.sin(a_im * dt)
    den = a_re * a_re + a_im * a_im
    n_re, n_im = lam_re - 1.0, lam_im
    coef_re = (n_re * a_re + n_im * a_im) / den
    coef_im = (n_im * a_re - n_re * a_im) / den
    b_re, b_im = b_re.astype(f32), b_im.astype(f32)
    bb_re = coef_re[..., None] * b_re - coef_im[..., None] * b_im
    bb_im = coef_re[..., None] * b_im + coef_im[..., None] * b_re
    c_re, c_im = c_re.astype(f32), c_im.astype(f32)
    chunk = min(SSM_CHUNK, T)
    n_ch = T // chunk
    lam_re_b = jnp.broadcast_to(lam_re, (Bn, chunk, SSM_GROUPS, SSM_STATE))
    lam_im_b = jnp.broadcast_to(lam_im, (Bn, chunk, SSM_GROUPS, SSM_STATE))

    def step(carry, u_c):
        hr, hi = carry
        br = jnp.einsum('btgc,gpc->btgp', u_c, bb_re)
        bi = jnp.einsum('btgc,gpc->btgp', u_c, bb_im)
        br = br.at[:, 0].add(lam_re * hr - lam_im * hi)
        bi = bi.at[:, 0].add(lam_re * hi + lam_im * hr)
        _, _, sr, si = lax.associative_scan(ssm_combine, (lam_re_b, lam_im_b, br, bi), axis=1)
        y = jnp.einsum('btgp,gcp->btgc', sr, c_re) - jnp.einsum('btgp,gcp->btgc', si, c_im)
        return (sr[:, -1], si[:, -1]), y

    u_chunks = ug.reshape(Bn, n_ch, chunk, SSM_GROUPS, SSM_CH).swapaxes(0, 1)
    (hr, hi), ys = lax.scan(step, (h_re.astype(f32), h_im.astype(f32)), u_chunks)
    y = ys.swapaxes(0, 1).reshape(Bn, T, D_MODEL) + d.astype(f32) * u
    z = jax.nn.gelu(y)
    out = (z * jax.nn.sigmoid(z @ w_gate.astype(f32))) @ w_out.astype(f32)
    return out.astype(xn.dtype), hr, hi


def conv_ffn(xn, buf, w_up, conv_w, conv_b, w_down):
    T = xn.shape[1]
    a, b = jnp.split(xn @ w_up, 2, axis=-1)
    a_ext = jnp.concatenate([buf.astype(a.dtype), a], axis=1)
    c = conv_b
    for j in range(CONV_W):
        c = c + conv_w[j] * a_ext[:, j:j + T]
    out = (jax.nn.silu(c) * b) @ w_down
    return out, a_ext[:, T:]


def nsa_project(xn, pos, w_in, q_gain, k_gain):
    Bn, T, _ = xn.shape
    proj = xn @ w_in
    q = proj[..., :D_MODEL].reshape(Bn, T, N_HEADS, HEAD_DIM)
    kv = proj[..., D_MODEL:D_MODEL + 6 * KV_DIM].reshape(Bn, T, 6, N_KV_HEADS, HEAD_DIM)
    gates = jax.nn.sigmoid(proj[..., D_MODEL + 6 * KV_DIM:].astype(jnp.float32)).reshape(Bn, T, 3, N_HEADS)
    q = rope(rmsnorm(q, q_gain), pos)
    k_c = rope(rmsnorm(kv[:, :, 0], k_gain[0]), pos)
    k_s = rope(rmsnorm(kv[:, :, 2], k_gain[1]), pos)
    k_w = rope(rmsnorm(kv[:, :, 4], k_gain[2]), pos)
    return q, gates, k_c, kv[:, :, 1], k_s, kv[:, :, 3], k_w, kv[:, :, 5]


def compress(rows, w1, b1, w2):
    Bn, T, G, Dh = rows.shape
    blk = rows.reshape(Bn, T // L_CMP, L_CMP, G, Dh)
    h = jax.nn.gelu(jnp.einsum('bclgd,lde->bcge', blk, w1) + b1)
    return h @ w2


def nsa_attend(q, gates, pos_q, kc, vc, cmp_end, gather_sel, kw, vw, pos_w):
    Bn, Tq = q.shape[0], q.shape[1]
    scale = HEAD_DIM ** -0.5
    qg = q.reshape(Bn, Tq, N_KV_HEADS, HEADS_PER_GROUP, HEAD_DIM).transpose(0, 2, 3, 1, 4)
    s_c = jnp.einsum('bghqd,bcgd->bghqc', qg, kc) * scale
    p_c = masked_softmax(s_c, cmp_end[None, :] <= pos_q[:, None])
    o_c = jnp.einsum('bghqc,bcgd->bghqd', p_c, vc)
    n_sel = kc.shape[1] // CMP_PER_SEL
    imp = p_c.sum(axis=2).reshape(Bn, N_KV_HEADS, Tq, n_sel, CMP_PER_SEL).sum(-1)
    blk = jnp.arange(n_sel)[None, :]
    qblk = (pos_q // L_SEL)[:, None]
    valid = blk * L_SEL <= pos_q[:, None]
    forced = (blk == 0) | (blk == qblk) | (blk == qblk - 1)
    score = jnp.where(valid & forced, FORCE_SCORE, jnp.where(valid, imp, NEG_INF))
    _, idx = lax.top_k(score, min(TOP_N, n_sel))
    n_top = idx.shape[-1]
    k_s, v_s = gather_sel(idx)
    k_s = k_s.reshape(Bn, N_KV_HEADS, Tq, n_top * L_SEL, HEAD_DIM)
    v_s = v_s.reshape(Bn, N_KV_HEADS, Tq, n_top * L_SEL, HEAD_DIM)
    key_pos = (idx[..., None] * L_SEL + jnp.arange(L_SEL)).reshape(Bn, N_KV_HEADS, Tq, n_top * L_SEL)
    s_s = jnp.einsum('bghqd,bgqkd->bghqk', qg, k_s) * scale
    p_s = masked_softmax(s_s, key_pos[:, :, None] <= pos_q[:, None])
    o_s = jnp.einsum('bghqk,bgqkd->bghqd', p_s, v_s)
    s_w = jnp.einsum('bghqd,bkgd->bghqk', qg, kw) * scale
    dpos = pos_q[:, None] - pos_w[None, :]
    m_w = (dpos >= 0) & (dpos < WINDOW) & (pos_w[None, :] >= 0)
    p_w = masked_softmax(s_w, m_w)
    o_w = jnp.einsum('bghqk,bkgd->bghqd', p_w, vw)
    g = gates.reshape(Bn, Tq, 3, N_KV_HEADS, HEADS_PER_GROUP).transpose(2, 0, 3, 4, 1)[..., None]
    o = g[0] * o_c + g[1] * o_s + g[2] * o_w
    return o.transpose(0, 3, 1, 2, 4).reshape(Bn, Tq, N_HEADS * HEAD_DIM)


def nsa_prompt(xn, w_in, q_gain, k_gain, ck_w1, ck_b1, ck_w2, cv_w1, cv_b1, cv_w2, w_out):
    Bn, T, _ = xn.shape
    pos = jnp.arange(T)
    q, gates, kc_r, vc_r, ks_r, vs_r, kw_r, vw_r = nsa_project(xn, pos, w_in, q_gain, k_gain)
    kc = compress(kc_r, ck_w1, ck_b1, ck_w2)
    vc = compress(vc_r, cv_w1, cv_b1, cv_w2)
    cmp_end = jnp.arange(T // L_CMP) * L_CMP + (L_CMP - 1)
    n_sel = T // L_SEL
    kb = ks_r.reshape(Bn, n_sel, L_SEL, N_KV_HEADS, HEAD_DIM).transpose(0, 3, 1, 2, 4)
    vb = vs_r.reshape(Bn, n_sel, L_SEL, N_KV_HEADS, HEAD_DIM).transpose(0, 3, 1, 2, 4)
    bi = jnp.arange(Bn)[:, None, None, None]
    gi = jnp.arange(N_KV_HEADS)[None, :, None, None]

    def gather_sel(idx):
        return kb[bi, gi, idx], vb[bi, gi, idx]

    pad = ((0, 0), (WINDOW, 0), (0, 0), (0, 0))
    kw_pad, vw_pad = jnp.pad(kw_r, pad), jnp.pad(vw_r, pad)
    qb = min(Q_BLOCK, T)

    def block(i):
        s0 = i * qb
        q_b = lax.dynamic_slice_in_dim(q, s0, qb, axis=1)
        g_b = lax.dynamic_slice_in_dim(gates, s0, qb, axis=1)
        kw_b = lax.dynamic_slice_in_dim(kw_pad, s0, WINDOW + qb, axis=1)
        vw_b = lax.dynamic_slice_in_dim(vw_pad, s0, WINDOW + qb, axis=1)
        pos_q = s0 + jnp.arange(qb)
        pos_w = s0 - WINDOW + jnp.arange(WINDOW + qb)
        return nsa_attend(q_b, g_b, pos_q, kc, vc, cmp_end, gather_sel, kw_b, vw_b, pos_w)

    o = lax.map(block, jnp.arange(T // qb))
    o = o.swapaxes(0, 1).reshape(Bn, T, D_MODEL)
    n_keep = min(WINDOW, T)
    rows = (kc_r, vc_r, ks_r, vs_r, kw_r[:, T - n_keep:], vw_r[:, T - n_keep:])
    return o.astype(xn.dtype) @ w_out, rows


def nsa_sample(xn, pool_kc, pool_vc, pool_ks, pool_vs, win_k, win_v, page_table,
               w_in, q_gain, k_gain, ck_w1, ck_b1, ck_w2, cv_w1, cv_b1, cv_w2, w_out):
    Bn, T, _ = xn.shape
    n_pages = page_table.shape[1]
    past = n_pages * PAGE_SIZE
    pos = past + jnp.arange(T)
    q, gates, kc_r, vc_r, ks_r, vs_r, kw_r, vw_r = nsa_project(xn, pos, w_in, q_gain, k_gain)
    new_pad = -(-T // L_SEL) * L_SEL
    pad_t = ((0, 0), (0, new_pad - T), (0, 0), (0, 0))

    def full_rows(pool, new):
        past_rows = pool[page_table].reshape(Bn, past, N_KV_HEADS, HEAD_DIM)
        return jnp.concatenate([past_rows, jnp.pad(new, pad_t).astype(pool.dtype)], axis=1)

    kc = compress(full_rows(pool_kc, kc_r), ck_w1, ck_b1, ck_w2)
    vc = compress(full_rows(pool_vc, vc_r), cv_w1, cv_b1, cv_w2)
    cmp_end = jnp.arange((past + new_pad) // L_CMP) * L_CMP + (L_CMP - 1)
    n_sel_past = past // L_SEL
    n_sel_new = new_pad // L_SEL
    bpp = PAGE_SIZE // L_SEL
    ks_view = pool_ks.reshape(pool_ks.shape[0], bpp, L_SEL, N_KV_HEADS, HEAD_DIM)
    vs_view = pool_vs.reshape(pool_vs.shape[0], bpp, L_SEL, N_KV_HEADS, HEAD_DIM)
    ks_new = jnp.pad(ks_r, pad_t).reshape(Bn, n_sel_new, L_SEL, N_KV_HEADS, HEAD_DIM).transpose(0, 3, 1, 2, 4)
    vs_new = jnp.pad(vs_r, pad_t).reshape(Bn, n_sel_new, L_SEL, N_KV_HEADS, HEAD_DIM).transpose(0, 3, 1, 2, 4)
    bi = jnp.arange(Bn)[:, None, None, None]
    gi = jnp.arange(N_KV_HEADS)[None, :, None, None]

    def gather_sel(idx):
        in_past = (idx < n_sel_past)[..., None, None]
        jp = jnp.minimum(idx, n_sel_past - 1)
        phys = page_table[bi, jp // bpp]
        sub = jp % bpp
        jn = jnp.clip(idx - n_sel_past, 0, n_sel_new - 1)
        k = jnp.where(in_past, ks_view[phys, sub, :, gi, :], ks_new[bi, gi, jn].astype(ks_view.dtype))
        v = jnp.where(in_past, vs_view[phys, sub, :, gi, :], vs_new[bi, gi, jn].astype(vs_view.dtype))
        return k, v

    n_win = win_k.shape[1]
    kw = jnp.concatenate([win_k, kw_r.astype(win_k.dtype)], axis=1)
    vw = jnp.concatenate([win_v, vw_r.astype(win_v.dtype)], axis=1)
    pos_w = past - n_win + jnp.arange(n_win + T)
    o = nsa_attend(q, gates, pos, kc, vc, cmp_end, gather_sel, kw, vw, pos_w)
    rows = (kc_r, vc_r, ks_r, vs_r, kw_r, vw_r)
    return o.astype(xn.dtype) @ w_out, rows


def setup_inputs(seed: int = 0) -> dict:
    key = jax.random.key(seed)
    keys = iter(jax.random.split(key, 64))
    f32 = jnp.float32

    def nrm(shape, scale):
        return scale * jax.random.normal(next(keys), shape, f32)

    n_pages = PAST_LEN // PAGE_SIZE
    n_used = DEC_BATCH * n_pages
    n_pool = n_used + n_used // 4
    win_buf = min(WINDOW, PAST_LEN)
    Ls, La = N_SSM_LAYERS, N_NSA_LAYERS
    pool_shape = (La, n_pool, PAGE_SIZE, N_KV_HEADS, HEAD_DIM)
    win_shape = (La, DEC_BATCH, win_buf, N_KV_HEADS, HEAD_DIM)
    ssm_shape = (Ls, DEC_BATCH, SSM_GROUPS, SSM_STATE)
    page_table = jax.random.permutation(next(keys), n_pool)[:n_used].reshape(DEC_BATCH, n_pages).astype(jnp.int32)
    n_idx = jnp.arange(SSM_STATE, dtype=f32)
    return {
        "x_prompt": nrm((BATCH, SEQ, D_MODEL), 1.0),
        "x_sample": nrm((DEC_BATCH, DEC_SEQ, D_MODEL), 1.0),
        "cache_k_cmp": nrm(pool_shape, 1.0),
        "cache_v_cmp": nrm(pool_shape, 1.0),
        "cache_k_slc": nrm(pool_shape, 1.0),
        "cache_v_slc": nrm(pool_shape, 1.0),
        "cache_k_win": nrm(win_shape, 1.0),
        "cache_v_win": nrm(win_shape, 1.0),
        "state_ssm_re": nrm(ssm_shape, 0.5),
        "state_ssm_im": nrm(ssm_shape, 0.5),
        "state_conv": nrm((DEPTH, DEC_BATCH, CONV_W - 1, D_FF), 1.0),
        "page_table": page_table,
        "norm_mix": 1.0 + nrm((DEPTH, D_MODEL), 0.1),
        "norm_ffn": 1.0 + nrm((DEPTH, D_MODEL), 0.1),
        "ssm_w_in": nrm((Ls, D_MODEL, D_MODEL), D_MODEL ** -0.5),
        "ssm_a_re": -0.5 + nrm((Ls, SSM_GROUPS, SSM_STATE), 0.01),
        "ssm_a_im": math.pi * n_idx + nrm((Ls, SSM_GROUPS, SSM_STATE), 0.01),
        "ssm_log_dt": jax.random.uniform(next(keys), (Ls, SSM_GROUPS), f32, math.log(DT_MIN), math.log(DT_MAX)),
        "ssm_b_re": nrm((Ls, SSM_GROUPS, SSM_STATE, SSM_CH), (2 * SSM_CH) ** -0.5),
        "ssm_b_im": nrm((Ls, SSM_GROUPS, SSM_STATE, SSM_CH), (2 * SSM_CH) ** -0.5),
        "ssm_c_re": nrm((Ls, SSM_GROUPS, SSM_CH, SSM_STATE), SSM_STATE ** -0.5),
        "ssm_c_im": nrm((Ls, SSM_GROUPS, SSM_CH, SSM_STATE), SSM_STATE ** -0.5),
        "ssm_d": 1.0 + nrm((Ls, D_MODEL), 0.1),
        "ssm_w_gate": nrm((Ls, D_MODEL, D_MODEL), D_MODEL ** -0.5),
        "ssm_w_out": nrm((Ls, D_MODEL, D_MODEL), D_MODEL ** -0.5),
        "nsa_w_in": nrm((La, D_MODEL, NSA_PROJ), D_MODEL ** -0.5),
        "nsa_q_gain": 1.0 + nrm((La, HEAD_DIM), 0.1),
        "nsa_k_gain": 1.0 + nrm((La, 3, HEAD_DIM), 0.1),
        "nsa_ck_w1": nrm((La, L_CMP, HEAD_DIM, HEAD_DIM), (L_CMP * HEAD_DIM) ** -0.5),
        "nsa_ck_b1": nrm((La, HEAD_DIM), 0.02),
        "nsa_ck_w2": nrm((La, HEAD_DIM, HEAD_DIM), HEAD_DIM ** -0.5),
        "nsa_cv_w1": nrm((La, L_CMP, HEAD_DIM, HEAD_DIM), (L_CMP * HEAD_DIM) ** -0.5),
        "nsa_cv_b1": nrm((La, HEAD_DIM), 0.02),
        "nsa_cv_w2": nrm((La, HEAD_DIM, HEAD_DIM), HEAD_DIM ** -0.5),
        "nsa_w_out": nrm((La, D_MODEL, D_MODEL), D_MODEL ** -0.5),
        "ffn_w_up": nrm((DEPTH, D_MODEL, 2 * D_FF), D_MODEL ** -0.5),
        "ffn_conv_w": nrm((DEPTH, CONV_W, D_FF), CONV_W ** -0.5),
        "ffn_conv_b": nrm((DEPTH, D_FF), 0.02),
        "ffn_w_down": nrm((DEPTH, D_FF, D_MODEL), D_FF ** -0.5),
    }


def reference(x_prompt, x_sample, cache_k_cmp, cache_v_cmp, cache_k_slc, cache_v_slc,
              cache_k_win, cache_v_win, state_ssm_re, state_ssm_im, state_conv, page_table,
              norm_mix, norm_ffn,
              ssm_w_in, ssm_a_re, ssm_a_im, ssm_log_dt, ssm_b_re, ssm_b_im, ssm_c_re, ssm_c_im,
              ssm_d, ssm_w_gate, ssm_w_out,
              nsa_w_in, nsa_q_gain, nsa_k_gain, nsa_ck_w1, nsa_ck_b1, nsa_ck_w2,
              nsa_cv_w1, nsa_cv_b1, nsa_cv_w2, nsa_w_out,
              ffn_w_up, ffn_conv_w, ffn_conv_b, ffn_w_down):
    xp, xs = x_prompt, x_sample
    bp = xp.shape[0]
    p_sre, p_sim, s_sre, s_sim = [], [], [], []
    p_rows = [[], [], [], [], [], []]
    s_rows = [[], [], [], [], [], []]
    p_cv, s_cv = [], []
    for i in range(DEPTH):
        li = i // 2
        hp = rmsnorm(xp, norm_mix[i])
        hs = rmsnorm(xs, norm_mix[i])
        if i % 2 == 0:
            sp = (ssm_w_in[li], ssm_a_re[li], ssm_a_im[li], ssm_log_dt[li], ssm_b_re[li], ssm_b_im[li],
                  ssm_c_re[li], ssm_c_im[li], ssm_d[li], ssm_w_gate[li], ssm_w_out[li])
            h0 = jnp.zeros((bp, SSM_GROUPS, SSM_STATE), jnp.float32)
            mp, hr, hi = ssm_mixer(hp, h0, h0, *sp)
            ms, sr, si = ssm_mixer(hs, state_ssm_re[li], state_ssm_im[li], *sp)
            p_sre.append(hr)
            p_sim.append(hi)
            s_sre.append(sr)
            s_sim.append(si)
        else:
            ap = (nsa_w_in[li], nsa_q_gain[li], nsa_k_gain[li], nsa_ck_w1[li], nsa_ck_b1[li], nsa_ck_w2[li],
                  nsa_cv_w1[li], nsa_cv_b1[li], nsa_cv_w2[li], nsa_w_out[li])
            mp, rp = nsa_prompt(hp, *ap)
            ms, rs = nsa_sample(hs, cache_k_cmp[li], cache_v_cmp[li], cache_k_slc[li], cache_v_slc[li],
                                cache_k_win[li], cache_v_win[li], page_table, *ap)
            for j in range(6):
                p_rows[j].append(rp[j])
                s_rows[j].append(rs[j])
        xp = xp + mp
        xs = xs + ms
        fw = (ffn_w_up[i], ffn_conv_w[i], ffn_conv_b[i], ffn_w_down[i])
        fp, cp = conv_ffn(rmsnorm(xp, norm_ffn[i]), jnp.zeros((bp, CONV_W - 1, D_FF), xp.dtype), *fw)
        fs, cs = conv_ffn(rmsnorm(xs, norm_ffn[i]), state_conv[i], *fw)
        xp = xp + fp
        xs = xs + fs
        p_cv.append(cp)
        s_cv.append(cs)
    p_ssm_re, p_ssm_im = jnp.stack(p_sre), jnp.stack(p_sim)
    s_ssm_re, s_ssm_im = jnp.stack(s_sre), jnp.stack(s_sim)
    p_k_cmp, p_v_cmp, p_k_slc, p_v_slc, p_k_win, p_v_win = [jnp.stack(r) for r in p_rows]
    s_k_cmp, s_v_cmp, s_k_slc, s_v_slc, s_k_win, s_v_win = [jnp.stack(r) for r in s_rows]
    p_conv, s_conv = jnp.stack(p_cv), jnp.stack(s_cv)
    return (xp, xs,
            p_ssm_re, p_ssm_im, p_k_cmp, p_v_cmp, p_k_slc, p_v_slc, p_k_win, p_v_win, p_conv,
            s_ssm_re, s_ssm_im, s_k_cmp, s_v_cmp, s_k_slc, s_v_slc, s_k_win, s_v_win, s_conv)
```

```python
import functools
import math

import jax
import jax.numpy as jnp
from jax import lax
from jax.experimental import pallas as pl
from jax.experimental.pallas import tpu as pltpu

F32 = jnp.float32
BF16 = jnp.bfloat16

NORM_EPS = 1e-6
SSM_CH = 16
SSM_STATE = 64
N_HEADS = 16
N_KV_HEADS = 4
HEAD_DIM = 64
L_CMP = 32
L_SEL = 64
TOP_N = 16
WINDOW = 512
PAGE_SIZE = 128
ROPE_THETA = 10000.0
NEG_INF = -1e30
FORCE_SCORE = 1e4
CONV_W = 3

LANES = 128
SUBLANES = 8
VMEM_LIMIT = 56 * 1024 * 1024


def _cparams(sem):
    return pltpu.CompilerParams(dimension_semantics=sem, vmem_limit_bytes=VMEM_LIMIT)


def _rms(x, g):
    return x * lax.rsqrt(jnp.mean(x * x, axis=-1, keepdims=True) + NORM_EPS) * g


def _ffn_kernel(x_ref, g_ref, wa_ref, wb_ref, cw_ref, cb_ref, wd_ref, init_ref,
                o_ref, buf_ref, xn_s, acc_s, aext_s, carry_s, *, shift, base, tm, nf):
    i = pl.program_id(1)
    j = pl.program_id(2)

    @pl.when(j == 0)
    def _():
        x = x_ref[...]
        xn_s[...] = _rms(x, g_ref[...]).astype(BF16)
        acc_s[...] = x

    xn = xn_s[...]
    a = jnp.dot(xn, wa_ref[...], preferred_element_type=F32)
    b = jnp.dot(xn, wb_ref[...], preferred_element_type=F32)

    @pl.when(i == 0)
    def _():
        aext_s[0:base, :] = init_ref[...]

    @pl.when(i > 0)
    def _():
        aext_s[0:base, :] = carry_s[j]

    aext_s[base:base + tm, :] = a
    a1 = aext_s[base - shift:base - shift + tm, :]
    a2 = aext_s[base - 2 * shift:base - 2 * shift + tm, :]
    cw = cw_ref[...]
    c = cb_ref[...] + cw[0:1, :] * a2
    c = c + cw[1:2, :] * a1
    c = c + cw[2:3, :] * a
    tail = aext_s[tm:tm + base, :]
    carry_s[j] = tail
    buf_ref[...] = tail
    h = (jax.nn.silu(c) * b).astype(BF16)
    acc_s[...] += jnp.dot(h, wd_ref[...], preferred_element_type=F32)

    @pl.when(j == nf - 1)
    def _():
        o_ref[...] = acc_s[...]


def _conv_ffn(x, init, g, w_up, conv_w, conv_b, w_down, *, shift, tm):
    S, R, D = x.shape
    Fd = w_down.shape[0]
    base = init.shape[1]
    nf = 2
    tf = Fd // nf
    assert tf * nf == Fd and tf % LANES == 0 and R % tm == 0 and base >= 2 * shift
    wu = w_up.astype(BF16)
    wd = w_down.astype(BF16)
    kern = functools.partial(_ffn_kernel, shift=shift, base=base, tm=tm, nf=nf)
    out, buf = pl.pallas_call(
        kern,
        grid=(S, R // tm, nf),
        in_specs=[
            pl.BlockSpec((None, tm, D), lambda s, i, j: (s, i, 0)),
            pl.BlockSpec((1, D), lambda s, i, j: (0, 0)),
            pl.BlockSpec((D, tf), lambda s, i, j: (0, j)),
            pl.BlockSpec((D, tf), lambda s, i, j: (0, nf + j)),
            pl.BlockSpec((CONV_W, tf), lambda s, i, j: (0, j)),
            pl.BlockSpec((1, tf), lambda s, i, j: (0, j)),
            pl.BlockSpec((tf, D), lambda s, i, j: (j, 0)),
            pl.BlockSpec((None, base, tf), lambda s, i, j: (s, 0, j)),
        ],
        out_specs=[
            pl.BlockSpec((None, tm, D), lambda s, i, j: (s, i, 0)),
            pl.BlockSpec((None, None, base, tf), lambda s, i, j: (s, i, 0, j)),
        ],
        out_shape=[jax.ShapeDtypeStruct((S, R, D), F32), jax.ShapeDtypeStruct((S, R // tm, base, Fd), F32)],
        scratch_shapes=[
            pltpu.VMEM((tm, D), BF16),
            pltpu.VMEM((tm, D), F32),
            pltpu.VMEM((tm + base, tf), F32),
            pltpu.VMEM((nf, base, tf), F32),
        ],
        compiler_params=_cparams(("arbitrary", "arbitrary", "arbitrary")),
        name="conv_ffn",
    )(x, g.reshape(1, D), wu, wu, conv_w, conv_b.reshape(1, Fd), wd, init)
    return out, buf[:, R // tm - 1]


def _s5_tables(a_re, a_im, log_dt, b_re, b_im, c_re, c_im):
    G, P = a_re.shape
    nt = G * SSM_CH // LANES
    gt = G // nt
    a_re, a_im = a_re.astype(F32), a_im.astype(F32)
    dt = jnp.exp(log_dt.astype(F32))[:, None]
    mag = jnp.exp(a_re * dt)
    lam_re, lam_im = mag * jnp.cos(a_im * dt), mag * jnp.sin(a_im * dt)
    den = a_re * a_re + a_im * a_im
    n_re, n_im = lam_re - 1.0, lam_im
    coef_re = (n_re * a_re + n_im * a_im) / den
    coef_im = (n_im * a_re - n_re * a_im) / den
    b_re, b_im = b_re.astype(F32), b_im.astype(F32)
    bb_re = coef_re[..., None] * b_re - coef_im[..., None] * b_im
    bb_im = coef_re[..., None] * b_im + coef_im[..., None] * b_re
    eye = jnp.eye(gt, dtype=F32)

    def in_proj(bb):
        v = bb.reshape(nt, gt, P, SSM_CH).transpose(0, 1, 3, 2)
        return jnp.einsum('igcp,gh->igchp', v, eye).reshape(nt, gt * SSM_CH, gt * P)

    def out_proj(c):
        v = c.astype(F32).reshape(nt, gt, SSM_CH, P)
        return jnp.einsum('igcp,gh->igphc', v, eye).reshape(nt, gt * P, gt * SSM_CH)

    wb = jnp.concatenate([in_proj(bb_re), in_proj(bb_im)], axis=2).astype(BF16)
    wc = jnp.concatenate([out_proj(c_re), -out_proj(c_im)], axis=1).astype(BF16)

    def cmul(xr, xi, yr, yi):
        return xr * yr - xi * yi, xr * yi + xi * yr

    l1 = (lam_re.reshape(1, G * P), lam_im.reshape(1, G * P))
    pows = [l1]
    for _ in range(SUBLANES - 1):
        pows.append(cmul(*pows[-1], *l1))
    lvl = jnp.concatenate([pows[0][0], pows[0][1], pows[1][0], pows[1][1], pows[3][0], pows[3][1],
                           jnp.zeros((2, G * P), F32)], axis=0)
    lp_re = jnp.concatenate([p[0] for p in pows], axis=0)
    lp_im = jnp.concatenate([p[1] for p in pows], axis=0)
    return wb, wc, lvl, lp_re, lp_im


def _s5_kernel(x_ref, g_ref, win_ref, wb_ref, wc_ref, lvl_ref, lpr_ref, lpi_ref, d_ref, wg_ref, wo_ref,
               ir_ref, ii_ref, o_ref, fr_ref, fi_ref, sre, sim, u_s, y_s, *, tc, slab, nt):
    ci = pl.program_id(1)
    x = x_ref[...]
    xn = _rms(x, g_ref[...]).astype(BF16)
    u = jnp.dot(xn, win_ref[...], preferred_element_type=F32)
    u_s[...] = u
    ub = u.astype(BF16)
    n2 = sre.shape[1] // nt
    for i in range(nt):
        bu = jnp.dot(ub[:, i * LANES:(i + 1) * LANES], wb_ref[i], preferred_element_type=F32)
        sre[:, i * n2:(i + 1) * n2] = bu[:, :n2]
        sim[:, i * n2:(i + 1) * n2] = bu[:, n2:]

    if slab == 1:
        @pl.when(ci == 0)
        def _():
            fr_ref[...] = ir_ref[...]
            fi_ref[...] = ii_ref[...]

        row = lax.broadcasted_iota(jnp.int32, (tc, n2), 0) % SUBLANES
        for i in range(nt):
            cs = slice(i * n2, (i + 1) * n2)
            xr, xi = sre[:, cs], sim[:, cs]
            for lv, dist in enumerate((1, 2, 4)):
                lr = lvl_ref[2 * lv:2 * lv + 1, cs]
                li = lvl_ref[2 * lv + 1:2 * lv + 2, cs]
                keep = row >= dist
                pr = jnp.where(keep, pltpu.roll(xr, dist, 0), 0.0)
                pi = jnp.where(keep, pltpu.roll(xi, dist, 0), 0.0)
                xr, xi = lr * pr - li * pi + xr, lr * pi + li * pr + xi
            sre[:, cs] = xr
            sim[:, cs] = xi

        nh = 2
        wh = sre.shape[1] // nh
        for hh in range(nh):
            cs = slice(hh * wh, (hh + 1) * wh)
            lpr, lpi = lpr_ref[:, cs], lpi_ref[:, cs]

            def body(k, carry, cs=cs, lpr=lpr, lpi=lpi):
                cr, cim = carry
                r0 = pl.multiple_of(k * SUBLANES, SUBLANES)
                hr = lpr * cr - lpi * cim + sre[pl.ds(r0, SUBLANES), cs]
                hi = lpr * cim + lpi * cr + sim[pl.ds(r0, SUBLANES), cs]
                sre[pl.ds(r0, SUBLANES), cs] = hr
                sim[pl.ds(r0, SUBLANES), cs] = hi
                return hr[SUBLANES - 1:SUBLANES, :], hi[SUBLANES - 1:SUBLANES, :]

            init = (fr_ref[SUBLANES - 1:SUBLANES, cs], fi_ref[SUBLANES - 1:SUBLANES, cs])
            lax.fori_loop(0, tc // SUBLANES, body, init, unroll=2)
        fr_ref[...] = sre[tc - SUBLANES:tc, :]
        fi_ref[...] = sim[tc - SUBLANES:tc, :]
    else:
        lr, li = lvl_ref[0:1, :], lvl_ref[1:2, :]
        hr, hi = ir_ref[...], ii_ref[...]
        for t in range(tc // slab):
            rs = slice(t * slab, (t + 1) * slab)
            hr, hi = lr * hr - li * hi + sre[rs, :], lr * hi + li * hr + sim[rs, :]
            sre[rs, :] = hr
            sim[rs, :] = hi
        fr_ref[...] = hr
        fi_ref[...] = hi

    for i in range(nt):
        cs = slice(i * n2, (i + 1) * n2)
        s_cat = jnp.concatenate([sre[:, cs].astype(BF16), sim[:, cs].astype(BF16)], axis=1)
        y_s[:, i * LANES:(i + 1) * LANES] = jnp.dot(s_cat, wc_ref[i], preferred_element_type=F32)
    z = jax.nn.gelu(y_s[...] + d_ref[...] * u_s[...])
    gate = jnp.dot(z.astype(BF16), wg_ref[...], preferred_element_type=F32)
    out = jnp.dot((z * jax.nn.sigmoid(gate)).astype(BF16), wo_ref[...], preferred_element_type=F32)
    o_ref[...] = x_ref[...] + out


def _s5_layer(x, init_re, init_im, g, w_in, tables, d, w_gate, w_out, *, tc, slab):
    S, R, D = x.shape
    wb, wc, lvl, lp_re, lp_im = tables
    nt = wb.shape[0]
    N = lvl.shape[1]
    hb = init_re.shape[1]
    assert R % tc == 0 and (slab == 1 or R == tc)
    kern = functools.partial(_s5_kernel, tc=tc, slab=slab, nt=nt)
    const2 = lambda s, c: (0, 0)
    const3 = lambda s, c: (0, 0, 0)
    out, fr, fi = pl.pallas_call(
        kern,
        grid=(S, R // tc),
        in_specs=[
            pl.BlockSpec((None, tc, D), lambda s, c: (s, c, 0)),
            pl.BlockSpec((1, D), const2),
            pl.BlockSpec((D, D), const2),
            pl.BlockSpec(wb.shape, const3),
            pl.BlockSpec(wc.shape, const3),
            pl.BlockSpec(lvl.shape, const2),
            pl.BlockSpec(lp_re.shape, const2),
            pl.BlockSpec(lp_im.shape, const2),
            pl.BlockSpec((1, D), const2),
            pl.BlockSpec((D, D), const2),
            pl.BlockSpec((D, D), const2),
            pl.BlockSpec((None, hb, N), lambda s, c: (s, 0, 0)),
            pl.BlockSpec((None, hb, N), lambda s, c: (s, 0, 0)),
        ],
        out_specs=[
            pl.BlockSpec((None, tc, D), lambda s, c: (s, c, 0)),
            pl.BlockSpec((None, hb, N), lambda s, c: (s, 0, 0)),
            pl.BlockSpec((None, hb, N), lambda s, c: (s, 0, 0)),
        ],
        out_shape=[jax.ShapeDtypeStruct((S, R, D), F32), jax.ShapeDtypeStruct((S, hb, N), F32),
                   jax.ShapeDtypeStruct((S, hb, N), F32)],
        scratch_shapes=[pltpu.VMEM((tc, N), F32), pltpu.VMEM((tc, N), F32),
                        pltpu.VMEM((tc, D), F32), pltpu.VMEM((tc, D), F32)],
        compiler_params=_cparams(("arbitrary", "arbitrary")),
        name="s5_layer",
    )(x, g.reshape(1, D), w_in.astype(BF16), wb, wc, lvl, lp_re, lp_im, d.reshape(1, D).astype(F32),
      w_gate.astype(BF16), w_out.astype(BF16), init_re, init_im)
    return out, fr, fi


D_Q = N_HEADS * HEAD_DIM
KV_DIM = N_KV_HEADS * HEAD_DIM
N_KINDS = 6
N_GATES = 3 * N_HEADS
HALF = HEAD_DIM // 2


def _rope_tables(pos):
    inv = jnp.power(ROPE_THETA, -jnp.arange(HALF, dtype=F32) / HALF)
    ang = inv[:, None] * pos.astype(F32)[None, :]
    return jnp.cos(ang), jnp.sin(ang)


def _nsa_proj_kernel(x_ref, g_ref, wt_ref, qg_ref, kg_ref, cos_ref, sin_ref,
                     q_ref, kv_ref, kvb_ref, gt_ref, *rest, tm, paged):
    if paged:
        kcp_ref, vcp_ref, pt_s = rest
    else:
        (pt_s,) = rest
    xn = _rms(x_ref[...], g_ref[...]).astype(BF16)
    pt_s[...] = lax.dot_general(wt_ref[...], xn, (((1,), (1,)), ((), ())), preferred_element_type=F32)
    cos, sin = cos_ref[...], sin_ref[...]

    def norm_rope(blk, gain):
        y = blk * lax.rsqrt(jnp.mean(blk * blk, axis=0, keepdims=True) + NORM_EPS) * gain
        x1, x2 = y[:HALF], y[HALF:]
        return x1 * cos - x2 * sin, x2 * cos + x1 * sin

    scale = HEAD_DIM ** -0.5
    for h in range(N_HEADS):
        r0 = h * HEAD_DIM
        r1, r2 = norm_rope(pt_s[r0:r0 + HEAD_DIM, :], qg_ref[...])
        q_ref[r0:r0 + HALF, :] = (r1 * scale).astype(BF16)
        q_ref[r0 + HALF:r0 + HEAD_DIM, :] = (r2 * scale).astype(BF16)
    for kind in range(N_KINDS):
        for gi in range(N_KV_HEADS):
            r0 = gi * HEAD_DIM
            blk = pt_s[D_Q + kind * KV_DIM + r0:D_Q + kind * KV_DIM + r0 + HEAD_DIM, :]
            if kind % 2 == 0:
                r1, r2 = norm_rope(blk, kg_ref[kind // 2])
                blk = jnp.concatenate([r1, r2], axis=0)
            kv_ref[kind, r0:r0 + HEAD_DIM, :] = blk
            if kind >= 2:
                kvb_ref[kind - 2, r0:r0 + HEAD_DIM, :] = blk.astype(BF16)
            elif paged:
                dst = kcp_ref if kind == 0 else vcp_ref
                for p in range(tm // PAGE_SIZE):
                    dst[p, gi] = blk[:, p * PAGE_SIZE:(p + 1) * PAGE_SIZE]
    g0 = D_Q + N_KINDS * KV_DIM
    gt_ref[...] = jax.nn.sigmoid(pt_s[g0:g0 + N_GATES, :])


def _nsa_project(x, pos, g, w_in, q_gain, k_gain, *, tm, paged):
    S, R, D = x.shape
    NP = w_in.shape[1]
    assert R % tm == 0 and (not paged or tm % PAGE_SIZE == 0)
    nr = R // tm
    wt = w_in.T.astype(BF16)
    cos, sin = _rope_tables(pos)
    qg = jnp.broadcast_to(q_gain.astype(F32)[:, None], (HEAD_DIM, tm))
    kg = jnp.broadcast_to(k_gain.astype(F32)[:, :, None], (3, HEAD_DIM, tm))
    out_specs = [
        pl.BlockSpec((None, D_Q, tm), lambda s, i: (s, 0, i)),
        pl.BlockSpec((None, N_KINDS, KV_DIM, tm), lambda s, i: (s, 0, 0, i)),
        pl.BlockSpec((None, 4, KV_DIM, tm), lambda s, i: (s, 0, 0, i)),
        pl.BlockSpec((None, N_GATES, tm), lambda s, i: (s, 0, i)),
    ]
    out_shape = [
        jax.ShapeDtypeStruct((S, D_Q, R), BF16),
        jax.ShapeDtypeStruct((S, N_KINDS, KV_DIM, R), F32),
        jax.ShapeDtypeStruct((S, 4, KV_DIM, R), BF16),
        jax.ShapeDtypeStruct((S, N_GATES, R), F32),
    ]
    if paged:
        ppt = tm // PAGE_SIZE
        for _ in range(2):
            out_specs.append(pl.BlockSpec((ppt, N_KV_HEADS, HEAD_DIM, PAGE_SIZE),
                                          lambda s, i: (s * nr + i, 0, 0, 0)))
            out_shape.append(jax.ShapeDtypeStruct((S * R // PAGE_SIZE, N_KV_HEADS, HEAD_DIM, PAGE_SIZE), F32))
    return pl.pallas_call(
        functools.partial(_nsa_proj_kernel, tm=tm, paged=paged),
        grid=(S, nr),
        in_specs=[
            pl.BlockSpec((None, tm, D), lambda s, i: (s, i, 0)),
            pl.BlockSpec((1, D), lambda s, i: (0, 0)),
            pl.BlockSpec((NP, D), lambda s, i: (0, 0)),
            pl.BlockSpec((HEAD_DIM, tm), lambda s, i: (0, 0)),
            pl.BlockSpec((3, HEAD_DIM, tm), lambda s, i: (0, 0, 0)),
            pl.BlockSpec((HALF, tm), lambda s, i: (0, i)),
            pl.BlockSpec((HALF, tm), lambda s, i: (0, i)),
        ],
        out_specs=out_specs,
        out_shape=out_shape,
        scratch_shapes=[pltpu.VMEM((NP, tm), F32)],
        compiler_params=_cparams(("arbitrary", "arbitrary")),
        name="nsa_project",
    )(x, g.reshape(1, D), wt, qg, kg, cos, sin)


PAGE_ROWS = N_KV_HEADS * HEAD_DIM
CMP_PER_PAGE = PAGE_SIZE // L_CMP


def _page_copy(pages_hbm, buf, sem, page, slot, p):
    return pltpu.make_async_copy(pages_hbm.at[page], buf.at[slot, pl.ds(p * PAGE_ROWS, PAGE_ROWS), :],
                                 sem.at[slot])


def _fetch_chunk(pt_ref, pages_hbm, buf, sem, n, slot, *, nchunks, npc):
    s = n // nchunks
    c = n % nchunks

    def start(p, carry):
        _page_copy(pages_hbm, buf, sem, pt_ref[s, c * npc + p], slot, p).start()
        return carry

    lax.fori_loop(0, npc, start, 0)


def _wait_chunk(pages_hbm, buf, sem, slot, *, npc):
    def wait(p, carry):
        _page_copy(pages_hbm, buf, sem, 0, slot, p).wait()
        return carry

    lax.fori_loop(0, npc, wait, 0)


def _stream_pages(pt_ref, pages_hbm, buf, sem, *, nchunks, npc, total):
    n = pl.program_id(0) * nchunks + pl.program_id(1)
    slot = n % 2

    @pl.when(n == 0)
    def _():
        _fetch_chunk(pt_ref, pages_hbm, buf, sem, n, slot, nchunks=nchunks, npc=npc)

    @pl.when(n + 1 < total)
    def _():
        _fetch_chunk(pt_ref, pages_hbm, buf, sem, n + 1, 1 - slot, nchunks=nchunks, npc=npc)

    _wait_chunk(pages_hbm, buf, sem, slot, npc=npc)
    return slot


def _compress_tables(w1, b1, w2):
    eye = jnp.eye(CMP_PER_PAGE, dtype=F32)
    m = jnp.einsum('lde,ck->dclke', w1.astype(F32), eye)
    m = m.reshape(HEAD_DIM // 2, 2 * PAGE_SIZE, CMP_PER_PAGE * HEAD_DIM).astype(BF16)
    b1t = jnp.tile(b1.astype(F32), CMP_PER_PAGE).reshape(1, CMP_PER_PAGE * HEAD_DIM)
    w2bd = jnp.einsum('ef,ck->cekf', w2.astype(F32), eye)
    w2bd = w2bd.reshape(CMP_PER_PAGE * HEAD_DIM, CMP_PER_PAGE * HEAD_DIM).astype(BF16)
    return m, b1t, w2bd


def _compress_kernel(pt_ref, pages_hbm, m_ref, b1_ref, w2_ref, o_ref, buf, sem, *, nchunks, npc, total):
    slot = _stream_pages(pt_ref, pages_hbm, buf, sem, nchunks=nchunks, npc=npc, total=total)
    nrow = npc * N_KV_HEADS

    def body(dp, acc):
        d0 = 2 * dp
        x0 = buf[slot, pl.ds(d0, nrow, stride=HEAD_DIM), :]
        x1 = buf[slot, pl.ds(d0 + 1, nrow, stride=HEAD_DIM), :]
        lhs = jnp.concatenate([x0, x1], axis=1).astype(BF16)
        return acc + jnp.dot(lhs, m_ref[dp], preferred_element_type=F32)

    acc = lax.fori_loop(0, HEAD_DIM // 2, body, jnp.zeros((nrow, CMP_PER_PAGE * HEAD_DIM), F32))
    h = jax.nn.gelu(acc + b1_ref[...])
    o_ref[...] = jnp.dot(h.astype(BF16), w2_ref[...], preferred_element_type=F32)


def _compress(pages, page_table, tables, *, npc):
    S, NPG = page_table.shape
    assert NPG % npc == 0
    nchunks = NPG // npc
    m, b1t, w2bd = tables
    nrow = npc * N_KV_HEADS
    nce = CMP_PER_PAGE * HEAD_DIM
    pages2 = pages.reshape(pages.shape[0], PAGE_ROWS, PAGE_SIZE)
    kern = functools.partial(_compress_kernel, nchunks=nchunks, npc=npc, total=S * nchunks)
    out = pl.pallas_call(
        kern,
        grid_spec=pltpu.PrefetchScalarGridSpec(
            num_scalar_prefetch=1,
            grid=(S, nchunks),
            in_specs=[
                pl.BlockSpec(memory_space=pl.ANY),
                pl.BlockSpec(m.shape, lambda s, c, pt: (0, 0, 0)),
                pl.BlockSpec(b1t.shape, lambda s, c, pt: (0, 0)),
                pl.BlockSpec(w2bd.shape, lambda s, c, pt: (0, 0)),
            ],
            out_specs=pl.BlockSpec((None, nrow, nce), lambda s, c, pt: (s, c, 0)),
            scratch_shapes=[pltpu.VMEM((2, npc * PAGE_ROWS, PAGE_SIZE), F32), pltpu.SemaphoreType.DMA((2,))],
        ),
        out_shape=jax.ShapeDtypeStruct((S, NPG * N_KV_HEADS, nce), F32),
        compiler_params=_cparams(("arbitrary", "arbitrary")),
        name="nsa_compress",
    )(page_table, pages2, m, b1t, w2bd)
    out = out.reshape(S, NPG, N_KV_HEADS, CMP_PER_PAGE, HEAD_DIM).transpose(0, 1, 3, 2, 4)
    return out.reshape(S, NPG * CMP_PER_PAGE, N_KV_HEADS, HEAD_DIM)


HPG = N_HEADS // N_KV_HEADS
CMP_PER_SEL = L_SEL // L_CMP
NT_DIMS = (((1,), (1,)), ((), ()))


def _masked_softmax(s, mask):
    s = jnp.where(mask, s, NEG_INF)
    e = jnp.exp(s - jnp.max(s, axis=-1, keepdims=True))
    return jnp.where(mask, e / jnp.sum(e, axis=-1, keepdims=True), 0.0)


def _select_blocks(imp, pos_q, n_sel):
    nq = imp.shape[0]
    jidx = lax.broadcasted_iota(jnp.int32, (nq, n_sel), 1)
    qblk = pos_q // L_SEL
    valid = jidx * L_SEL <= pos_q
    forced = (jidx == 0) | (jidx == qblk) | (jidx == qblk - 1)
    score = jnp.where(valid & forced, FORCE_SCORE, jnp.where(valid, imp, NEG_INF))
    sct = score.T
    jrow = lax.broadcasted_iota(jnp.int32, (n_sel, nq), 0)
    sel = jnp.zeros((n_sel, nq), F32)
    for _ in range(min(TOP_N, n_sel)):
        top = jnp.max(sct, axis=0, keepdims=True)
        first = jnp.min(jnp.where(sct == top, jrow, n_sel), axis=0, keepdims=True)
        hit = jrow == first
        sct = jnp.where(hit, -jnp.inf, sct)
        sel = jnp.where(hit, 1.0, sel)
    return sel


def _cmp_block_of_column(ncol):
    c = lax.broadcasted_iota(jnp.int32, (1, ncol), 1)
    half = ncol // CMP_PER_SEL
    return jnp.where(c < half, CMP_PER_SEL * c, CMP_PER_SEL * (c - half) + 1)


Q_BLOCK = 128


def _attn_prompt_kernel(q_ref, kc_ref, vc_ref, ks_ref, vs_ref, kw_ref, vw_ref, gt_ref, e_ref, o_ref,
                        *, tk, T):
    qi = pl.program_id(2)
    nq = Q_BLOCK
    nc = kc_ref.shape[1]
    n_sel = nc // CMP_PER_SEL
    qrows = q_ref[...].astype(F32).T
    q = jnp.concatenate([qrows[:, h * HEAD_DIM:(h + 1) * HEAD_DIM] for h in range(HPG)], axis=0)
    q = q.astype(BF16)
    pos_q = qi * nq + lax.broadcasted_iota(jnp.int32, (nq, 1), 0)

    s_c = jnp.dot(q, kc_ref[...], preferred_element_type=F32).reshape(HPG, nq, nc)
    cmask = (_cmp_block_of_column(nc) * L_CMP + (L_CMP - 1)) <= pos_q
    p_c = _masked_softmax(s_c, cmask[None])
    o_c = jnp.dot(p_c.reshape(HPG * nq, nc).astype(BF16), vc_ref[...], preferred_element_type=F32)
    imp = p_c[0]
    for h in range(1, HPG):
        imp = imp + p_c[h]
    imp = imp[:, :n_sel] + imp[:, n_sel:]
    sel = _select_blocks(imp, pos_q, n_sel).T.astype(BF16)

    def sel_body(kt, carry):
        m, l, acc = carry
        k0 = pl.multiple_of(kt * tk, tk)
        s = jnp.dot(q, ks_ref[:, pl.ds(k0, tk)], preferred_element_type=F32).reshape(HPG, nq, tk)
        chosen = jnp.dot(sel, e_ref[:, pl.ds(k0, tk)], preferred_element_type=F32)
        kpos = k0 + lax.broadcasted_iota(jnp.int32, (nq, tk), 1)
        msk = ((chosen > 0.5) & (kpos <= pos_q))[None]
        s = jnp.where(msk, s, NEG_INF)
        m_new = jnp.maximum(m, jnp.max(s, axis=-1, keepdims=True))
        alpha = jnp.exp(m - m_new)
        p = jnp.where(msk, jnp.exp(s - m_new), 0.0)
        l = alpha * l + jnp.sum(p, axis=-1, keepdims=True)
        pv = lax.dot_general(p.reshape(HPG * nq, tk).astype(BF16), vs_ref[:, pl.ds(k0, tk)], NT_DIMS,
                             preferred_element_type=F32)
        return m_new, l, alpha * acc + pv.reshape(HPG, nq, HEAD_DIM)

    nkt = (qi * nq + nq + tk - 1) // tk
    init = (jnp.full((HPG, nq, 1), NEG_INF, F32), jnp.zeros((HPG, nq, 1), F32),
            jnp.zeros((HPG, nq, HEAD_DIM), F32))
    _, l_s, acc_s = lax.fori_loop(0, nkt, sel_body, init)
    o_s = acc_s / jnp.where(l_s > 0.0, l_s, 1.0)

    wl = min(WINDOW + nq, T)
    w0 = pl.multiple_of(jnp.maximum(qi * nq + nq - wl, 0), LANES)
    s_w = jnp.dot(q, kw_ref[:, pl.ds(w0, wl)], preferred_element_type=F32).reshape(HPG, nq, wl)
    dpos = pos_q - (w0 + lax.broadcasted_iota(jnp.int32, (nq, wl), 1))
    p_w = _masked_softmax(s_w, ((dpos >= 0) & (dpos < WINDOW))[None])
    o_w = lax.dot_general(p_w.reshape(HPG * nq, wl).astype(BF16), vw_ref[:, pl.ds(w0, wl)], NT_DIMS,
                          preferred_element_type=F32).reshape(HPG, nq, HEAD_DIM)

    gt = gt_ref[...]
    o_c = o_c.reshape(HPG, nq, HEAD_DIM)
    outs = []
    for h in range(HPG):
        outs.append(gt[:, h:h + 1] * o_c[h] + gt[:, HPG + h:HPG + h + 1] * o_s[h]
                    + gt[:, 2 * HPG + h:2 * HPG + h + 1] * o_w[h])
    o_ref[...] = jnp.concatenate(outs, axis=1).astype(BF16)


def _block_expansion(n_sel, n_keys):
    return (jnp.arange(n_keys)[None, :] // L_SEL == jnp.arange(n_sel)[:, None]).astype(BF16)


def _cmp_layouts(kc, vc):
    order = jnp.concatenate([jnp.arange(0, kc.shape[1], CMP_PER_SEL), jnp.arange(1, kc.shape[1], CMP_PER_SEL)])
    kct = kc[:, order].transpose(0, 2, 3, 1).astype(BF16)
    vcr = vc[:, order].transpose(0, 2, 1, 3).astype(BF16)
    return kct, vcr


def _attend_prompt(qT, kvbT, gT, kc, vc, *, tk):
    B, _, T = qT.shape
    G = N_KV_HEADS
    nq = Q_BLOCK
    assert T % tk == 0 and T % nq == 0 and tk % L_SEL == 0
    nc = T // L_CMP
    n_sel = T // L_SEL
    kct, vcr = _cmp_layouts(kc, vc)
    q4 = qT.reshape(B, G, HPG * HEAD_DIM, T)
    kv5 = kvbT.reshape(B, 4, G, HEAD_DIM, T)
    gates = gT.reshape(B, 3, G, HPG, T).transpose(0, 2, 4, 1, 3).reshape(B, G, T, 3 * HPG)
    e = _block_expansion(n_sel, T)
    kv_spec = lambda kind: pl.BlockSpec((None, None, None, HEAD_DIM, T), lambda b, g, i: (b, kind, g, 0, 0))
    return pl.pallas_call(
        functools.partial(_attn_prompt_kernel, tk=tk, T=T),
        grid=(B, G, T // nq),
        in_specs=[
            pl.BlockSpec((None, None, HPG * HEAD_DIM, nq), lambda b, g, i: (b, g, 0, i)),
            pl.BlockSpec((None, None, HEAD_DIM, nc), lambda b, g, i: (b, g, 0, 0)),
            pl.BlockSpec((None, None, nc, HEAD_DIM), lambda b, g, i: (b, g, 0, 0)),
            kv_spec(0), kv_spec(1), kv_spec(2), kv_spec(3),
            pl.BlockSpec((None, None, nq, 3 * HPG), lambda b, g, i: (b, g, i, 0)),
            pl.BlockSpec((n_sel, T), lambda b, g, i: (0, 0)),
        ],
        out_specs=pl.BlockSpec((None, nq, HPG * HEAD_DIM), lambda b, g, i: (b, i, g)),
        out_shape=jax.ShapeDtypeStruct((B, T, D_Q), BF16),
        compiler_params=_cparams(("arbitrary", "arbitrary", "arbitrary")),
        name="nsa_attend_prompt",
    )(q4, kct, vcr, kv5, kv5, kv5, kv5, gates, e)


def _attn_sample_kernel(pt_ref, q_ref, kc_ref, vc_ref, kwc_ref, vwc_ref, kwn_ref, vwn_ref, ksn_ref, vsn_ref,
                        gt_ref, e_ref, dm_ref, ks_hbm, vs_hbm, o_ref,
                        kbuf, vbuf, ksem, vsem, s_s, sel_s, m_s, l_s, acc_s, oc_s, ow_s,
                        *, nchunks, npc, total, past, tq):
    c = pl.program_id(1)
    kslot = _stream_pages(pt_ref, ks_hbm, kbuf, ksem, nchunks=nchunks, npc=npc, total=total)
    vslot = _stream_pages(pt_ref, vs_hbm, vbuf, vsem, nchunks=nchunks, npc=npc, total=total)
    q = q_ref[...]
    nr = q.shape[0]
    ngt = N_KV_HEADS * tq
    pos_q = past + lax.broadcasted_iota(jnp.int32, (nr, 1), 0) % tq
    n_sel = sel_s.shape[1]
    sel_past = past // L_SEL

    def fold(o_full):
        o = o_full * dm_ref[...]
        out = o[:, 0:HEAD_DIM]
        for gi in range(1, N_KV_HEADS):
            out = out + o[:, gi * HEAD_DIM:(gi + 1) * HEAD_DIM]
        return out

    def nt(p, vt):
        return lax.dot_general(p.astype(BF16), vt.astype(BF16), NT_DIMS, preferred_element_type=F32)

    @pl.when(c == 0)
    def _():
        nc = kc_ref.shape[1]
        s_c = jnp.dot(q, kc_ref[...], preferred_element_type=F32)
        cmask = (_cmp_block_of_column(nc) * L_CMP + (L_CMP - 1)) <= pos_q
        p_c = _masked_softmax(s_c, cmask)
        oc_s[...] = fold(nt(p_c, vc_ref[...]))
        imp = p_c[0:ngt]
        for h in range(1, HPG):
            imp = imp + p_c[h * ngt:(h + 1) * ngt]
        imp = imp[:, :n_sel] + imp[:, n_sel:]
        pad = LANES - ngt
        imp = jnp.concatenate([imp, jnp.zeros((pad, n_sel), F32)], axis=0)
        pos_pad = past + lax.broadcasted_iota(jnp.int32, (LANES, 1), 0) % tq
        sel = _select_blocks(imp, pos_pad, n_sel).T[0:ngt]
        sel_s[...] = jnp.concatenate([sel] * HPG, axis=0)

        nwc = kwc_ref.shape[1]
        nwn = kwn_ref.shape[1]
        s_w = jnp.concatenate([jnp.dot(q, kwc_ref[...].astype(BF16), preferred_element_type=F32),
                               jnp.dot(q, kwn_ref[...].astype(BF16), preferred_element_type=F32)], axis=1)
        lane = lax.broadcasted_iota(jnp.int32, (nr, nwc + nwn), 1)
        pos_w = jnp.where(lane < nwc, past - nwc + lane, past + lane - nwc)
        dpos = pos_q - pos_w
        p_w = _masked_softmax(s_w, (dpos >= 0) & (dpos < WINDOW) & (pos_w >= 0))
        ow_s[...] = fold(nt(p_w[:, :nwc], vwc_ref[...]) + nt(p_w[:, nwc:], vwn_ref[...]))

        nsn = ksn_ref.shape[1]
        s_n = jnp.dot(q, ksn_ref[...].astype(BF16), preferred_element_type=F32)
        kpos = past + lax.broadcasted_iota(jnp.int32, (nr, nsn), 1)
        msk = (sel_s[:, sel_past:sel_past + 1] > 0.5) & (kpos <= pos_q)
        s_n = jnp.where(msk, s_n, NEG_INF)
        m0 = jnp.max(s_n, axis=-1, keepdims=True)
        p_n = jnp.where(msk, jnp.exp(s_n - m0), 0.0)
        m_s[...] = m0
        l_s[...] = jnp.sum(p_n, axis=-1, keepdims=True)
        acc_s[...] = nt(p_n, vsn_ref[...])

    def qk(p, carry):
        r0 = pl.multiple_of(p * PAGE_ROWS, PAGE_ROWS)
        c0 = pl.multiple_of(p * PAGE_SIZE, PAGE_SIZE)
        s_s[:, pl.ds(c0, PAGE_SIZE)] = jnp.dot(q, kbuf[kslot, pl.ds(r0, PAGE_ROWS), :].astype(BF16),
                                                preferred_element_type=F32)
        return carry

    lax.fori_loop(0, npc, qk, 0)
    nk = npc * PAGE_SIZE
    sel_c = sel_s[:, pl.ds(pl.multiple_of(c * (nk // L_SEL), LANES), nk // L_SEL)]
    chosen = jnp.dot(sel_c.astype(BF16), e_ref[...], preferred_element_type=F32)
    kpos = c * nk + lax.broadcasted_iota(jnp.int32, (nr, nk), 1)
    msk = (chosen > 0.5) & (kpos <= pos_q)
    s = jnp.where(msk, s_s[...], NEG_INF)
    m_old = m_s[...]
    m_new = jnp.maximum(m_old, jnp.max(s, axis=-1, keepdims=True))
    alpha = jnp.exp(m_old - m_new)
    p = jnp.where(msk, jnp.exp(s - m_new), 0.0)
    l_s[...] = alpha * l_s[...] + jnp.sum(p, axis=-1, keepdims=True)
    m_s[...] = m_new
    s_s[...] = p

    def pv(p_, acc):
        r0 = pl.multiple_of(p_ * PAGE_ROWS, PAGE_ROWS)
        c0 = pl.multiple_of(p_ * PAGE_SIZE, PAGE_SIZE)
        return acc + nt(s_s[:, pl.ds(c0, PAGE_SIZE)], vbuf[vslot, pl.ds(r0, PAGE_ROWS), :])

    acc = lax.fori_loop(0, npc, pv, jnp.zeros((nr, PAGE_ROWS), F32))
    acc_s[...] = alpha * acc_s[...] + acc

    @pl.when(c == nchunks - 1)
    def _():
        l = l_s[...]
        o_sel = fold(acc_s[...]) / jnp.where(l > 0.0, l, 1.0)
        gt = gt_ref[...]
        o_ref[...] = gt[:, 0:1] * oc_s[...] + gt[:, 1:2] * o_sel + gt[:, 2:3] * ow_s[...]


def _attend_sample(qT, kvT, gT, kc_all, vc_all, win_k, win_v, pool_ks, pool_vs, page_table, *, npc):
    Bs, NPG = page_table.shape
    tq = qT.shape[1] // Bs
    G = N_KV_HEADS
    past = NPG * PAGE_SIZE
    nchunks = NPG // npc
    nk = npc * PAGE_SIZE
    assert NPG % npc == 0 and nk // L_SEL == LANES
    nr = HPG * G * tq
    eye = jnp.eye(G, dtype=qT.dtype)
    q5 = qT.reshape(G, HPG, HEAD_DIM, tq, Bs).transpose(4, 1, 0, 3, 2)
    q_bd = jnp.einsum('bhgtd,gk->bhgtkd', q5, eye).reshape(Bs, nr, G * HEAD_DIM)
    dm = jnp.broadcast_to(jnp.eye(G, dtype=F32)[None, :, None, :, None],
                          (HPG, G, tq, G, HEAD_DIM)).reshape(nr, G * HEAD_DIM)
    gates = gT.reshape(3, G, HPG, tq, Bs).transpose(4, 2, 1, 3, 0).reshape(Bs, nr, 3)

    def new_rows(kind):
        r = kvT[kind].reshape(G * HEAD_DIM, tq, Bs).transpose(2, 0, 1)
        return jnp.pad(r, ((0, 0), (0, 0), (0, LANES - tq)))

    nc = kc_all.shape[1]
    half = nc // CMP_PER_SEL
    n_sel = -(-(half) // LANES) * LANES
    assert past // L_SEL < n_sel

    def cmp_fm(x):
        xt = x.transpose(0, 2, 3, 1).reshape(Bs, G * HEAD_DIM, nc)
        padw = ((0, 0), (0, 0), (0, n_sel - half))
        return jnp.concatenate([jnp.pad(xt[:, :, 0::CMP_PER_SEL], padw),
                                jnp.pad(xt[:, :, 1::CMP_PER_SEL], padw)], axis=2).astype(BF16)

    n_win = win_k.shape[-1]
    wk = win_k.reshape(Bs, G * HEAD_DIM, n_win)
    wv = win_v.reshape(Bs, G * HEAD_DIM, n_win)
    e = _block_expansion(nk // L_SEL, nk)
    ks2 = pool_ks.reshape(pool_ks.shape[0], PAGE_ROWS, PAGE_SIZE)
    vs2 = pool_vs.reshape(pool_vs.shape[0], PAGE_ROWS, PAGE_SIZE)
    per_b = lambda shape: pl.BlockSpec((None,) + shape, lambda b, c, pt: (b, 0, 0))
    kern = functools.partial(_attn_sample_kernel, nchunks=nchunks, npc=npc, total=Bs * nchunks, past=past, tq=tq)
    o = pl.pallas_call(
        kern,
        grid_spec=pltpu.PrefetchScalarGridSpec(
            num_scalar_prefetch=1,
            grid=(Bs, nchunks),
            in_specs=[
                per_b((nr, G * HEAD_DIM)),
                per_b((G * HEAD_DIM, 2 * n_sel)), per_b((G * HEAD_DIM, 2 * n_sel)),
                per_b((G * HEAD_DIM, n_win)), per_b((G * HEAD_DIM, n_win)),
                per_b((G * HEAD_DIM, LANES)), per_b((G * HEAD_DIM, LANES)),
                per_b((G * HEAD_DIM, LANES)), per_b((G * HEAD_DIM, LANES)),
                per_b((nr, 3)),
                pl.BlockSpec(e.shape, lambda b, c, pt: (0, 0)),
                pl.BlockSpec(dm.shape, lambda b, c, pt: (0, 0)),
                pl.BlockSpec(memory_space=pl.ANY),
                pl.BlockSpec(memory_space=pl.ANY),
            ],
            out_specs=pl.BlockSpec((None, nr, HEAD_DIM), lambda b, c, pt: (b, 0, 0)),
            scratch_shapes=[
                pltpu.VMEM((2, npc * PAGE_ROWS, PAGE_SIZE), F32), pltpu.VMEM((2, npc * PAGE_ROWS, PAGE_SIZE), F32),
                pltpu.SemaphoreType.DMA((2,)), pltpu.SemaphoreType.DMA((2,)),
                pltpu.VMEM((nr, nk), F32), pltpu.VMEM((nr, n_sel), F32),
                pltpu.VMEM((nr, 1), F32), pltpu.VMEM((nr, 1), F32), pltpu.VMEM((nr, G * HEAD_DIM), F32),
                pltpu.VMEM((nr, HEAD_DIM), F32), pltpu.VMEM((nr, HEAD_DIM), F32),
            ],
        ),
        out_shape=jax.ShapeDtypeStruct((Bs, nr, HEAD_DIM), F32),
        compiler_params=_cparams(("arbitrary", "arbitrary")),
        name="nsa_attend_sample",
    )(page_table, q_bd, cmp_fm(kc_all), cmp_fm(vc_all), wk, wv, new_rows(4), new_rows(5), new_rows(2), new_rows(3),
      gates, e, dm, ks2, vs2)
    o = o.reshape(Bs, HPG, G, tq, HEAD_DIM).transpose(3, 0, 2, 1, 4)
    return o.reshape(tq * Bs, D_Q).astype(BF16)


def _out_proj_kernel(a_ref, w_ref, x_ref, o_ref):
    o_ref[...] = x_ref[...] + jnp.dot(a_ref[...], w_ref[...], preferred_element_type=F32)


def _out_proj(a, w, x, *, tm):
    R, Kd = a.shape
    N = w.shape[1]
    assert R % tm == 0
    return pl.pallas_call(
        _out_proj_kernel,
        grid=(R // tm,),
        in_specs=[pl.BlockSpec((tm, Kd), lambda i: (i, 0)), pl.BlockSpec((Kd, N), lambda i: (0, 0)),
                  pl.BlockSpec((tm, N), lambda i: (i, 0))],
        out_specs=pl.BlockSpec((tm, N), lambda i: (i, 0)),
        out_shape=jax.ShapeDtypeStruct((R, N), F32),
        compiler_params=_cparams(("arbitrary",)),
        name="out_proj",
    )(a, w.astype(BF16), x)


PROMPT_ROW_TILE = 512
S5_CHUNK = 256
SEL_KEY_TILE = 512
SAMPLE_PAGES_PER_CHUNK = 64


def _feature_major(cache):
    nd = cache.ndim
    return jnp.moveaxis(cache, nd - 3, nd - 1)


def _nsa_prompt_layer(x, g, w_in, q_gain, k_gain, ck, cv, w_out):
    B, T, D = x.shape
    G = N_KV_HEADS
    qT, kvT, kvbT, gT, kcp, vcp = _nsa_project(x, jnp.arange(T), g, w_in, q_gain, k_gain,
                                               tm=PROMPT_ROW_TILE, paged=True)
    npg = T // PAGE_SIZE
    ptab = jnp.arange(B * npg, dtype=jnp.int32).reshape(B, npg)
    kc = _compress(kcp, ptab, _compress_tables(*ck), npc=npg)
    vc = _compress(vcp, ptab, _compress_tables(*cv), npc=npg)
    o = _attend_prompt(qT, kvbT, gT, kc, vc, tk=SEL_KEY_TILE)
    y = _out_proj(o.reshape(B * T, D_Q), w_out, x.reshape(B * T, D), tm=PROMPT_ROW_TILE).reshape(B, T, D)
    rows = [kvT[:, k].reshape(B, G, HEAD_DIM, T).transpose(0, 3, 1, 2) for k in range(N_KINDS)]
    n_keep = min(WINDOW, T)
    rows[4] = rows[4][:, T - n_keep:]
    rows[5] = rows[5][:, T - n_keep:]
    return y, rows


def _nsa_sample_layer(x_tm, Bs, pools, win_k, win_v, page_table, g, w_in, q_gain, k_gain, ck, cv, w_out):
    _, R, D = x_tm.shape
    G = N_KV_HEADS
    tq = R // Bs
    npg = page_table.shape[1]
    past = npg * PAGE_SIZE
    pos = past + jnp.arange(R) // Bs
    qT, kvT, _, gT = _nsa_project(x_tm, pos, g, w_in, q_gain, k_gain, tm=R, paged=False)
    qT, kvT, gT = qT[0], kvT[0], gT[0]
    new_pad = -(-tq // L_SEL) * L_SEL
    assert new_pad <= PAGE_SIZE
    new_tab = jnp.arange(Bs, dtype=jnp.int32).reshape(1, Bs)

    def cmp_all(pool, kind, tabs):
        past_blocks = _compress(_feature_major(pool), page_table, tabs, npc=SAMPLE_PAGES_PER_CHUNK)
        page = kvT[kind].reshape(G, HEAD_DIM, tq, Bs).transpose(3, 0, 1, 2)
        page = jnp.pad(page, ((0, 0), (0, 0), (0, 0), (0, PAGE_SIZE - tq)))
        new_blocks = _compress(page, new_tab, tabs, npc=Bs).reshape(Bs, CMP_PER_PAGE, G, HEAD_DIM)
        return jnp.concatenate([past_blocks, new_blocks[:, :new_pad // L_CMP]], axis=1)

    kc_all = cmp_all(pools[0], 0, _compress_tables(*ck))
    vc_all = cmp_all(pools[1], 1, _compress_tables(*cv))
    o = _attend_sample(qT, kvT, gT, kc_all, vc_all, _feature_major(win_k), _feature_major(win_v),
                       _feature_major(pools[2]), _feature_major(pools[3]), page_table,
                       npc=SAMPLE_PAGES_PER_CHUNK)
    y = _out_proj(o, w_out, x_tm[0], tm=R)[None]
    rows = [kvT[k].reshape(G, HEAD_DIM, tq, Bs).transpose(3, 2, 0, 1) for k in range(N_KINDS)]
    return y, rows


def kernel(x_prompt, x_sample, cache_k_cmp, cache_v_cmp, cache_k_slc, cache_v_slc, cache_k_win, cache_v_win,
           state_ssm_re, state_ssm_im, state_conv, page_table, norm_mix, norm_ffn,
           ssm_w_in, ssm_a_re, ssm_a_im, ssm_log_dt, ssm_b_re, ssm_b_im, ssm_c_re, ssm_c_im,
           ssm_d, ssm_w_gate, ssm_w_out,
           nsa_w_in, nsa_q_gain, nsa_k_gain, nsa_ck_w1, nsa_ck_b1, nsa_ck_w2,
           nsa_cv_w1, nsa_cv_b1, nsa_cv_w2, nsa_w_out,
           ffn_w_up, ffn_conv_w, ffn_conv_b, ffn_w_down):
    B, T, D = x_prompt.shape
    Bs, Tq, _ = x_sample.shape
    depth = norm_mix.shape[0]
    Fd = ffn_w_down.shape[1]
    G, P = ssm_a_re.shape[1:]
    xp = x_prompt
    xs = x_sample.transpose(1, 0, 2).reshape(1, Tq * Bs, D)
    p_sre, p_sim, s_sre, s_sim, p_cv, s_cv = [], [], [], [], [], []
    p_rows = [[] for _ in range(N_KINDS)]
    s_rows = [[] for _ in range(N_KINDS)]
    for i in range(depth):
        li = i // 2
        if i % 2 == 0:
            tabs = _s5_tables(ssm_a_re[li], ssm_a_im[li], ssm_log_dt[li], ssm_b_re[li], ssm_b_im[li],
                              ssm_c_re[li], ssm_c_im[li])
            sp = (norm_mix[i], ssm_w_in[li], tabs, ssm_d[li], ssm_w_gate[li], ssm_w_out[li])
            z = jnp.zeros((B, SUBLANES, G * P), F32)
            xp, fr, fi = _s5_layer(xp, z, z, *sp, tc=min(S5_CHUNK, T), slab=1)
            p_sre.append(fr[:, SUBLANES - 1].reshape(B, G, P))
            p_sim.append(fi[:, SUBLANES - 1].reshape(B, G, P))
            xs, fr, fi = _s5_layer(xs, state_ssm_re[li].reshape(1, Bs, G * P).astype(F32),
                                   state_ssm_im[li].reshape(1, Bs, G * P).astype(F32), *sp, tc=Tq * Bs, slab=Bs)
            s_sre.append(fr.reshape(Bs, G, P))
            s_sim.append(fi.reshape(Bs, G, P))
        else:
            ap = (norm_mix[i], nsa_w_in[li], nsa_q_gain[li], nsa_k_gain[li],
                  (nsa_ck_w1[li], nsa_ck_b1[li], nsa_ck_w2[li]), (nsa_cv_w1[li], nsa_cv_b1[li], nsa_cv_w2[li]),
                  nsa_w_out[li])
            xp, rp = _nsa_prompt_layer(xp, *ap)
            xs, rs = _nsa_sample_layer(xs, Bs, (cache_k_cmp[li], cache_v_cmp[li], cache_k_slc[li], cache_v_slc[li]),
                                       cache_k_win[li], cache_v_win[li], page_table, *ap)
            for j in range(N_KINDS):
                p_rows[j].append(rp[j])
                s_rows[j].append(rs[j])
        fw = (norm_ffn[i], ffn_w_up[i], ffn_conv_w[i], ffn_conv_b[i], ffn_w_down[i])
        xp, cp = _conv_ffn(xp, jnp.zeros((B, SUBLANES, Fd), F32), *fw, shift=1, tm=min(PROMPT_ROW_TILE, T))
        p_cv.append(cp[:, SUBLANES - (CONV_W - 1):])
        init = state_conv[i].astype(F32).transpose(1, 0, 2).reshape(1, (CONV_W - 1) * Bs, Fd)
        xs, cs = _conv_ffn(xs, init, *fw, shift=Bs, tm=Tq * Bs)
        s_cv.append(cs.reshape(CONV_W - 1, Bs, Fd).transpose(1, 0, 2))
    st = jnp.stack
    y_sample = xs.reshape(Tq, Bs, D).transpose(1, 0, 2)
    return (xp, y_sample,
            st(p_sre), st(p_sim), *[st(r) for r in p_rows], st(p_cv),
            st(s_sre), st(s_sim), *[st(r) for r in s_rows], st(s_cv))
```

```python
import functools
import math

import jax
import jax.numpy as jnp
from jax import lax
from jax.experimental import pallas as pl
from jax.experimental.pallas import tpu as pltpu

F32 = jnp.float32
BF16 = jnp.bfloat16

NORM_EPS = 1e-6
SSM_CH = 16
SSM_STATE = 64
N_HEADS = 16
N_KV_HEADS = 4
HEAD_DIM = 64
L_CMP = 32
L_SEL = 64
TOP_N = 16
WINDOW = 512
PAGE_SIZE = 128
ROPE_THETA = 10000.0
NEG_INF = -1e30
FORCE_SCORE = 1e4
CONV_W = 3

LANES = 128
SUBLANES = 8
VMEM_LIMIT = 56 * 1024 * 1024


def _cparams(sem):
    return pltpu.CompilerParams(dimension_semantics=sem, vmem_limit_bytes=VMEM_LIMIT)


def _rms(x, g):
    return x * lax.rsqrt(jnp.mean(x * x, axis=-1, keepdims=True) + NORM_EPS) * g


def _ffn_kernel(x_ref, g_ref, wa_ref, wb_ref, cw_ref, cb_ref, wd_ref, init_ref,
                o_ref, buf_ref, xn_s, acc_s, aext_s, carry_s, *, shift, base, tm, nf):
    i = pl.program_id(1)
    j = pl.program_id(2)

    @pl.when(j == 0)
    def _():
        x = x_ref[...]
        xn_s[...] = _rms(x, g_ref[...]).astype(BF16)
        acc_s[...] = x

    xn = xn_s[...]
    a = jnp.dot(xn, wa_ref[...], preferred_element_type=F32)
    b = jnp.dot(xn, wb_ref[...], preferred_element_type=F32)

    @pl.when(i == 0)
    def _():
        aext_s[0:base, :] = init_ref[...]

    @pl.when(i > 0)
    def _():
        aext_s[0:base, :] = carry_s[j]

    aext_s[base:base + tm, :] = a
    a1 = aext_s[base - shift:base - shift + tm, :]
    a2 = aext_s[base - 2 * shift:base - 2 * shift + tm, :]
    cw = cw_ref[...]
    c = cb_ref[...] + cw[0:1, :] * a2
    c = c + cw[1:2, :] * a1
    c = c + cw[2:3, :] * a
    tail = aext_s[tm:tm + base, :]
    carry_s[j] = tail
    buf_ref[...] = tail
    h = (jax.nn.silu(c) * b).astype(BF16)
    acc_s[...] += jnp.dot(h, wd_ref[...], preferred_element_type=F32)

    @pl.when(j == nf - 1)
    def _():
        o_ref[...] = acc_s[...]


def _conv_ffn(x, init, g, w_up, conv_w, conv_b, w_down, *, shift, tm):
    S, R, D = x.shape
    Fd = w_down.shape[0]
    base = init.shape[1]
    nf = 2
    tf = Fd // nf
    assert tf * nf == Fd and tf % LANES == 0 and R % tm == 0 and base >= 2 * shift
    wu = w_up.astype(BF16)
    wd = w_down.astype(BF16)
    kern = functools.partial(_ffn_kernel, shift=shift, base=base, tm=tm, nf=nf)
    out, buf = pl.pallas_call(
        kern,
        grid=(S, R // tm, nf),
        in_specs=[
            pl.BlockSpec((None, tm, D), lambda s, i, j: (s, i, 0)),
            pl.BlockSpec((1, D), lambda s, i, j: (0, 0)),
            pl.BlockSpec((D, tf), lambda s, i, j: (0, j)),
            pl.BlockSpec((D, tf), lambda s, i, j: (0, nf + j)),
            pl.BlockSpec((CONV_W, tf), lambda s, i, j: (0, j)),
            pl.BlockSpec((1, tf), lambda s, i, j: (0, j)),
            pl.BlockSpec((tf, D), lambda s, i, j: (j, 0)),
            pl.BlockSpec((None, base, tf), lambda s, i, j: (s, 0, j)),
        ],
        out_specs=[
            pl.BlockSpec((None, tm, D), lambda s, i, j: (s, i, 0)),
            pl.BlockSpec((None, None, base, tf), lambda s, i, j: (s, i, 0, j)),
        ],
        out_shape=[jax.ShapeDtypeStruct((S, R, D), F32), jax.ShapeDtypeStruct((S, R // tm, base, Fd), F32)],
        scratch_shapes=[
            pltpu.VMEM((tm, D), BF16),
            pltpu.VMEM((tm, D), F32),
            pltpu.VMEM((tm + base, tf), F32),
            pltpu.VMEM((nf, base, tf), F32),
        ],
        compiler_params=_cparams(("arbitrary", "arbitrary", "arbitrary")),
        name="conv_ffn",
    )(x, g.reshape(1, D), wu, wu, conv_w, conv_b.reshape(1, Fd), wd, init)
    return out, buf[:, R // tm - 1]


def _s5_tables(a_re, a_im, log_dt, b_re, b_im, c_re, c_im):
    G, P = a_re.shape
    nt = G * SSM_CH // LANES
    gt = G // nt
    a_re, a_im = a_re.astype(F32), a_im.astype(F32)
    dt = jnp.exp(log_dt.astype(F32))[:, None]
    mag = jnp.exp(a_re * dt)
    lam_re, lam_im = mag * jnp.cos(a_im * dt), mag * jnp.sin(a_im * dt)
    den = a_re * a_re + a_im * a_im
    n_re, n_im = lam_re - 1.0, lam_im
    coef_re = (n_re * a_re + n_im * a_im) / den
    coef_im = (n_im * a_re - n_re * a_im) / den
    b_re, b_im = b_re.astype(F32), b_im.astype(F32)
    bb_re = coef_re[..., None] * b_re - coef_im[..., None] * b_im
    bb_im = coef_re[..., None] * b_im + coef_im[..., None] * b_re
    eye = jnp.eye(gt, dtype=F32)

    def in_proj(bb):
        v = bb.reshape(nt, gt, P, SSM_CH).transpose(0, 1, 3, 2)
        return jnp.einsum('igcp,gh->igchp', v, eye).reshape(nt, gt * SSM_CH, gt * P)

    def out_proj(c):
        v = c.astype(F32).reshape(nt, gt, SSM_CH, P)
        return jnp.einsum('igcp,gh->igphc', v, eye).reshape(nt, gt * P, gt * SSM_CH)

    wb = jnp.concatenate([in_proj(bb_re), in_proj(bb_im)], axis=2).astype(BF16)
    wc = jnp.concatenate([out_proj(c_re), -out_proj(c_im)], axis=1).astype(BF16)

    def cmul(xr, xi, yr, yi):
        return xr * yr - xi * yi, xr * yi + xi * yr

    l1 = (lam_re.reshape(1, G * P), lam_im.reshape(1, G * P))
    pows = [l1]
    for _ in range(SUBLANES - 1):
        pows.append(cmul(*pows[-1], *l1))
    lvl = jnp.concatenate([pows[0][0], pows[0][1], pows[1][0], pows[1][1], pows[3][0], pows[3][1],
                           jnp.zeros((2, G * P), F32)], axis=0)
    lp_re = jnp.concatenate([p[0] for p in pows], axis=0)
    lp_im = jnp.concatenate([p[1] for p in pows], axis=0)
    return wb, wc, lvl, lp_re, lp_im


def _s5_kernel(x_ref, g_ref, win_ref, wb_ref, wc_ref, lvl_ref, lpr_ref, lpi_ref, d_ref, wg_ref, wo_ref,
               ir_ref, ii_ref, o_ref, fr_ref, fi_ref, sre, sim, u_s, y_s, *, tc, slab, nt):
    ci = pl.program_id(1)
    x = x_ref[...]
    xn = _rms(x, g_ref[...]).astype(BF16)
    u = jnp.dot(xn, win_ref[...], preferred_element_type=F32)
    u_s[...] = u
    ub = u.astype(BF16)
    n2 = sre.shape[1] // nt
    for i in range(nt):
        bu = jnp.dot(ub[:, i * LANES:(i + 1) * LANES], wb_ref[i], preferred_element_type=F32)
        sre[:, i * n2:(i + 1) * n2] = bu[:, :n2]
        sim[:, i * n2:(i + 1) * n2] = bu[:, n2:]

    if slab == 1:
        @pl.when(ci == 0)
        def _():
            fr_ref[...] = ir_ref[...]
            fi_ref[...] = ii_ref[...]

        row = lax.broadcasted_iota(jnp.int32, (tc, n2), 0) % SUBLANES
        for i in range(nt):
            cs = slice(i * n2, (i + 1) * n2)
            xr, xi = sre[:, cs], sim[:, cs]
            for lv, dist in enumerate((1, 2, 4)):
                lr = lvl_ref[2 * lv:2 * lv + 1, cs]
                li = lvl_ref[2 * lv + 1:2 * lv + 2, cs]
                keep = row >= dist
                pr = jnp.where(keep, pltpu.roll(xr, dist, 0), 0.0)
                pi = jnp.where(keep, pltpu.roll(xi, dist, 0), 0.0)
                xr, xi = lr * pr - li * pi + xr, lr * pi + li * pr + xi
            sre[:, cs] = xr
            sim[:, cs] = xi

        nh = 2
        wh = sre.shape[1] // nh
        for hh in range(nh):
            cs = slice(hh * wh, (hh + 1) * wh)
            lpr, lpi = lpr_ref[:, cs], lpi_ref[:, cs]

            def body(k, carry, cs=cs, lpr=lpr, lpi=lpi):
                cr, cim = carry
                r0 = pl.multiple_of(k * SUBLANES, SUBLANES)
                hr = lpr * cr - lpi * cim + sre[pl.ds(r0, SUBLANES), cs]
                hi = lpr * cim + lpi * cr + sim[pl.ds(r0, SUBLANES), cs]
                sre[pl.ds(r0, SUBLANES), cs] = hr
                sim[pl.ds(r0, SUBLANES), cs] = hi
                return hr[SUBLANES - 1:SUBLANES, :], hi[SUBLANES - 1:SUBLANES, :]

            init = (fr_ref[SUBLANES - 1:SUBLANES, cs], fi_ref[SUBLANES - 1:SUBLANES, cs])
            lax.fori_loop(0, tc // SUBLANES, body, init, unroll=2)
        fr_ref[...] = sre[tc - SUBLANES:tc, :]
        fi_ref[...] = sim[tc - SUBLANES:tc, :]
    else:
        lr, li = lvl_ref[0:1, :], lvl_ref[1:2, :]
        hr, hi = ir_ref[...], ii_ref[...]
        for t in range(tc // slab):
            rs = slice(t * slab, (t + 1) * slab)
            hr, hi = lr * hr - li * hi + sre[rs, :], lr * hi + li * hr + sim[rs, :]
            sre[rs, :] = hr
            sim[rs, :] = hi
        fr_ref[...] = hr
        fi_ref[...] = hi

    for i in range(nt):
        cs = slice(i * n2, (i + 1) * n2)
        s_cat = jnp.concatenate([sre[:, cs].astype(BF16), sim[:, cs].astype(BF16)], axis=1)
        y_s[:, i * LANES:(i + 1) * LANES] = jnp.dot(s_cat, wc_ref[i], preferred_element_type=F32)
    z = jax.nn.gelu(y_s[...] + d_ref[...] * u_s[...])
    gate = jnp.dot(z.astype(BF16), wg_ref[...], preferred_element_type=F32)
    out = jnp.dot((z * jax.nn.sigmoid(gate)).astype(BF16), wo_ref[...], preferred_element_type=F32)
    o_ref[...] = x_ref[...] + out


def _s5_layer(x, init_re, init_im, g, w_in, tables, d, w_gate, w_out, *, tc, slab):
    S, R, D = x.shape
    wb, wc, lvl, lp_re, lp_im = tables
    nt = wb.shape[0]
    N = lvl.shape[1]
    hb = init_re.shape[1]
    assert R % tc == 0 and (slab == 1 or R == tc)
    kern = functools.partial(_s5_kernel, tc=tc, slab=slab, nt=nt)
    const2 = lambda s, c: (0, 0)
    const3 = lambda s, c: (0, 0, 0)
    out, fr, fi = pl.pallas_call(
        kern,
        grid=(S, R // tc),
        in_specs=[
            pl.BlockSpec((None, tc, D), lambda s, c: (s, c, 0)),
            pl.BlockSpec((1, D), const2),
            pl.BlockSpec((D, D), const2),
            pl.BlockSpec(wb.shape, const3),
            pl.BlockSpec(wc.shape, const3),
            pl.BlockSpec(lvl.shape, const2),
            pl.BlockSpec(lp_re.shape, const2),
            pl.BlockSpec(lp_im.shape, const2),
            pl.BlockSpec((1, D), const2),
            pl.BlockSpec((D, D), const2),
            pl.BlockSpec((D, D), const2),
            pl.BlockSpec((None, hb, N), lambda s, c: (s, 0, 0)),
            pl.BlockSpec((None, hb, N), lambda s, c: (s, 0, 0)),
        ],
        out_specs=[
            pl.BlockSpec((None, tc, D), lambda s, c: (s, c, 0)),
            pl.BlockSpec((None, hb, N), lambda s, c: (s, 0, 0)),
            pl.BlockSpec((None, hb, N), lambda s, c: (s, 0, 0)),
        ],
        out_shape=[jax.ShapeDtypeStruct((S, R, D), F32), jax.ShapeDtypeStruct((S, hb, N), F32),
                   jax.ShapeDtypeStruct((S, hb, N), F32)],
        scratch_shapes=[pltpu.VMEM((tc, N), F32), pltpu.VMEM((tc, N), F32),
                        pltpu.VMEM((tc, D), F32), pltpu.VMEM((tc, D), F32)],
        compiler_params=_cparams(("arbitrary", "arbitrary")),
        name="s5_layer",
    )(x, g.reshape(1, D), w_in.astype(BF16), wb, wc, lvl, lp_re, lp_im, d.reshape(1, D).astype(F32),
      w_gate.astype(BF16), w_out.astype(BF16), init_re, init_im)
    return out, fr, fi


D_Q = N_HEADS * HEAD_DIM
KV_DIM = N_KV_HEADS * HEAD_DIM
N_KINDS = 6
N_GATES = 3 * N_HEADS
HALF = HEAD_DIM // 2


def _rope_tables(pos):
    inv = jnp.power(ROPE_THETA, -jnp.arange(HALF, dtype=F32) / HALF)
    ang = inv[:, None] * pos.astype(F32)[None, :]
    return jnp.cos(ang), jnp.sin(ang)


def _nsa_proj_kernel(x_ref, g_ref, wt_ref, qg_ref, kg_ref, cos_ref, sin_ref,
                     q_ref, kv_ref, kvb_ref, gt_ref, *rest, tm, paged):
    if paged:
        kcp_ref, vcp_ref, pt_s = rest
    else:
        (pt_s,) = rest
    xn = _rms(x_ref[...], g_ref[...]).astype(BF16)
    pt_s[...] = lax.dot_general(wt_ref[...], xn, (((1,), (1,)), ((), ())), preferred_element_type=F32)
    cos, sin = cos_ref[...], sin_ref[...]

    def norm_rope(blk, gain):
        y = blk * lax.rsqrt(jnp.mean(blk * blk, axis=0, keepdims=True) + NORM_EPS) * gain
        x1, x2 = y[:HALF], y[HALF:]
        return x1 * cos - x2 * sin, x2 * cos + x1 * sin

    scale = HEAD_DIM ** -0.5
    for h in range(N_HEADS):
        r0 = h * HEAD_DIM
        r1, r2 = norm_rope(pt_s[r0:r0 + HEAD_DIM, :], qg_ref[...])
        q_ref[r0:r0 + HALF, :] = (r1 * scale).astype(BF16)
        q_ref[r0 + HALF:r0 + HEAD_DIM, :] = (r2 * scale).astype(BF16)
    for kind in range(N_KINDS):
        for gi in range(N_KV_HEADS):
            r0 = gi * HEAD_DIM
            blk = pt_s[D_Q + kind * KV_DIM + r0:D_Q + kind * KV_DIM + r0 + HEAD_DIM, :]
            if kind % 2 == 0:
                r1, r2 = norm_rope(blk, kg_ref[kind // 2])
                blk = jnp.concatenate([r1, r2], axis=0)
            kv_ref[kind, r0:r0 + HEAD_DIM, :] = blk
            if kind >= 2:
                kvb_ref[kind - 2, r0:r0 + HEAD_DIM, :] = blk.astype(BF16)
            elif paged:
                dst = kcp_ref if kind == 0 else vcp_ref
                for p in range(tm // PAGE_SIZE):
                    dst[p, gi] = blk[:, p * PAGE_SIZE:(p + 1) * PAGE_SIZE]
    g0 = D_Q + N_KINDS * KV_DIM
    gt_ref[...] = jax.nn.sigmoid(pt_s[g0:g0 + N_GATES, :])


def _nsa_project(x, pos, g, w_in, q_gain, k_gain, *, tm, paged):
    S, R, D = x.shape
    NP = w_in.shape[1]
    assert R % tm == 0 and (not paged or tm % PAGE_SIZE == 0)
    nr = R // tm
    wt = w_in.T.astype(BF16)
    cos, sin = _rope_tables(pos)
    qg = jnp.broadcast_to(q_gain.astype(F32)[:, None], (HEAD_DIM, tm))
    kg = jnp.broadcast_to(k_gain.astype(F32)[:, :, None], (3, HEAD_DIM, tm))
    out_specs = [
        pl.BlockSpec((None, D_Q, tm), lambda s, i: (s, 0, i)),
        pl.BlockSpec((None, N_KINDS, KV_DIM, tm), lambda s, i: (s, 0, 0, i)),
        pl.BlockSpec((None, 4, KV_DIM, tm), lambda s, i: (s, 0, 0, i)),
        pl.BlockSpec((None, N_GATES, tm), lambda s, i: (s, 0, i)),
    ]
    out_shape = [
        jax.ShapeDtypeStruct((S, D_Q, R), BF16),
        jax.ShapeDtypeStruct((S, N_KINDS, KV_DIM, R), F32),
        jax.ShapeDtypeStruct((S, 4, KV_DIM, R), BF16),
        jax.ShapeDtypeStruct((S, N_GATES, R), F32),
    ]
    if paged:
        ppt = tm // PAGE_SIZE
        for _ in range(2):
            out_specs.append(pl.BlockSpec((ppt, N_KV_HEADS, HEAD_DIM, PAGE_SIZE),
                                          lambda s, i: (s * nr + i, 0, 0, 0)))
            out_shape.append(jax.ShapeDtypeStruct((S * R // PAGE_SIZE, N_KV_HEADS, HEAD_DIM, PAGE_SIZE), F32))
    return pl.pallas_call(
        functools.partial(_nsa_proj_kernel, tm=tm, paged=paged),
        grid=(S, nr),
        in_specs=[
            pl.BlockSpec((None, tm, D), lambda s, i: (s, i, 0)),
            pl.BlockSpec((1, D), lambda s, i: (0, 0)),
            pl.BlockSpec((NP, D), lambda s, i: (0, 0)),
            pl.BlockSpec((HEAD_DIM, tm), lambda s, i: (0, 0)),
            pl.BlockSpec((3, HEAD_DIM, tm), lambda s, i: (0, 0, 0)),
            pl.BlockSpec((HALF, tm), lambda s, i: (0, i)),
            pl.BlockSpec((HALF, tm), lambda s, i: (0, i)),
        ],
        out_specs=out_specs,
        out_shape=out_shape,
        scratch_shapes=[pltpu.VMEM((NP, tm), F32)],
        compiler_params=_cparams(("arbitrary", "arbitrary")),
        name="nsa_project",
    )(x, g.reshape(1, D), wt, qg, kg, cos, sin)


PAGE_ROWS = N_KV_HEADS * HEAD_DIM
CMP_PER_PAGE = PAGE_SIZE // L_CMP


def _page_copies(pages_hbm, buf, sem, page, slot, p, by_feature):
    if not by_feature:
        return [pltpu.make_async_copy(pages_hbm.at[page], buf.at[slot, pl.ds(p * PAGE_ROWS, PAGE_ROWS), :],
                                      sem.at[slot])]
    return [pltpu.make_async_copy(pages_hbm.at[page, pl.ds(g * HEAD_DIM, HEAD_DIM), :],
                                  buf.at[slot, :, p * N_KV_HEADS + g, :], sem.at[slot])
            for g in range(N_KV_HEADS)]


def _fetch_chunk(pt_ref, pages_hbm, buf, sem, n, slot, *, nchunks, npc, by_feature):
    s = n // nchunks
    c = n % nchunks

    def start(p, carry):
        for cp in _page_copies(pages_hbm, buf, sem, pt_ref[s, c * npc + p], slot, p, by_feature):
            cp.start()
        return carry

    lax.fori_loop(0, npc, start, 0)


def _wait_chunk(pages_hbm, buf, sem, slot, *, npc, by_feature):
    def wait(p, carry):
        for cp in _page_copies(pages_hbm, buf, sem, 0, slot, p, by_feature):
            cp.wait()
        return carry

    lax.fori_loop(0, npc, wait, 0)


def _stream_pages(pt_ref, pages_hbm, buf, sem, *, nchunks, npc, total, by_feature=False):
    n = pl.program_id(0) * nchunks + pl.program_id(1)
    slot = n % 2
    kw = dict(nchunks=nchunks, npc=npc, by_feature=by_feature)

    @pl.when(n == 0)
    def _():
        _fetch_chunk(pt_ref, pages_hbm, buf, sem, n, slot, **kw)

    @pl.when(n + 1 < total)
    def _():
        _fetch_chunk(pt_ref, pages_hbm, buf, sem, n + 1, 1 - slot, **kw)

    _wait_chunk(pages_hbm, buf, sem, slot, npc=npc, by_feature=by_feature)
    return slot


def _compress_tables(w1, b1, w2):
    eye = jnp.eye(CMP_PER_PAGE, dtype=F32)
    m = jnp.einsum('lde,ck->dclke', w1.astype(F32), eye)
    m = m.reshape(HEAD_DIM * PAGE_SIZE, CMP_PER_PAGE * HEAD_DIM).astype(BF16)
    b1t = jnp.tile(b1.astype(F32), CMP_PER_PAGE).reshape(1, CMP_PER_PAGE * HEAD_DIM)
    w2bd = jnp.einsum('ef,ck->cekf', w2.astype(F32), eye)
    w2bd = w2bd.reshape(CMP_PER_PAGE * HEAD_DIM, CMP_PER_PAGE * HEAD_DIM).astype(BF16)
    return m, b1t, w2bd


def _compress_kernel(pt_ref, pages_hbm, m_ref, b1_ref, w2_ref, o_ref, buf, sem, lhs_s, *, nchunks, npc, total):
    slot = _stream_pages(pt_ref, pages_hbm, buf, sem, nchunks=nchunks, npc=npc, total=total, by_feature=True)
    for d in range(HEAD_DIM):
        lhs_s[:, d * PAGE_SIZE:(d + 1) * PAGE_SIZE] = buf[slot, d].astype(BF16)
    acc = jnp.dot(lhs_s[...], m_ref[...], preferred_element_type=F32)
    h = jax.nn.gelu(acc + b1_ref[...])
    o_ref[...] = jnp.dot(h.astype(BF16), w2_ref[...], preferred_element_type=F32)


def _compress(pages, page_table, tables, *, npc):
    S, NPG = page_table.shape
    assert NPG % npc == 0
    nchunks = NPG // npc
    m, b1t, w2bd = tables
    nrow = npc * N_KV_HEADS
    nce = CMP_PER_PAGE * HEAD_DIM
    pages2 = pages.reshape(pages.shape[0], PAGE_ROWS, PAGE_SIZE)
    kern = functools.partial(_compress_kernel, nchunks=nchunks, npc=npc, total=S * nchunks)
    out = pl.pallas_call(
        kern,
        grid_spec=pltpu.PrefetchScalarGridSpec(
            num_scalar_prefetch=1,
            grid=(S, nchunks),
            in_specs=[
                pl.BlockSpec(memory_space=pl.ANY),
                pl.BlockSpec(m.shape, lambda s, c, pt: (0, 0)),
                pl.BlockSpec(b1t.shape, lambda s, c, pt: (0, 0)),
                pl.BlockSpec(w2bd.shape, lambda s, c, pt: (0, 0)),
            ],
            out_specs=pl.BlockSpec((None, nrow, nce), lambda s, c, pt: (s, c, 0)),
            scratch_shapes=[pltpu.VMEM((2, HEAD_DIM, nrow, PAGE_SIZE), F32), pltpu.SemaphoreType.DMA((2,)),
                            pltpu.VMEM((nrow, HEAD_DIM * PAGE_SIZE), BF16)],
        ),
        out_shape=jax.ShapeDtypeStruct((S, NPG * N_KV_HEADS, nce), F32),
        compiler_params=_cparams(("arbitrary", "arbitrary")),
        name="nsa_compress",
    )(page_table, pages2, m, b1t, w2bd)
    out = out.reshape(S, NPG, N_KV_HEADS, CMP_PER_PAGE, HEAD_DIM).transpose(0, 1, 3, 2, 4)
    return out.reshape(S, NPG * CMP_PER_PAGE, N_KV_HEADS, HEAD_DIM)


HPG = N_HEADS // N_KV_HEADS
CMP_PER_SEL = L_SEL // L_CMP
NT_DIMS = (((1,), (1,)), ((), ()))


def _masked_softmax(s, mask):
    s = jnp.where(mask, s, NEG_INF)
    e = jnp.exp(s - jnp.max(s, axis=-1, keepdims=True))
    return jnp.where(mask, e / jnp.sum(e, axis=-1, keepdims=True), 0.0)


def _select_blocks(imp, pos_q, n_sel):
    nq = imp.shape[0]
    jidx = lax.broadcasted_iota(jnp.int32, (nq, n_sel), 1)
    qblk = pos_q // L_SEL
    valid = jidx * L_SEL <= pos_q
    forced = (jidx == 0) | (jidx == qblk) | (jidx == qblk - 1)
    score = jnp.where(valid & forced, FORCE_SCORE, jnp.where(valid, imp, NEG_INF))
    sct = score.T
    jrow = lax.broadcasted_iota(jnp.int32, (n_sel, nq), 0)
    sel = jnp.zeros((n_sel, nq), F32)
    for _ in range(min(TOP_N, n_sel)):
        top = jnp.max(sct, axis=0, keepdims=True)
        first = jnp.min(jnp.where(sct == top, jrow, n_sel), axis=0, keepdims=True)
        hit = jrow == first
        sct = jnp.where(hit, -jnp.inf, sct)
        sel = jnp.where(hit, 1.0, sel)
    return sel


def _cmp_block_of_column(ncol):
    c = lax.broadcasted_iota(jnp.int32, (1, ncol), 1)
    half = ncol // CMP_PER_SEL
    return jnp.where(c < half, CMP_PER_SEL * c, CMP_PER_SEL * (c - half) + 1)


Q_BLOCK = 128
GROUPS_PER_STEP = 2
BF16_SUBLANES = 2 * SUBLANES


M_FLOOR = 0.1 * NEG_INF


def _bias_softmax(s, bias):
    s = s + bias
    e = jnp.exp(s - jnp.maximum(jnp.max(s, axis=-1, keepdims=True), M_FLOOR))
    l = jnp.sum(e, axis=-1, keepdims=True)
    return e, 1.0 / jnp.where(l > 0.0, l, 1.0)


def _attn_prompt_kernel(q_ref, kc_ref, vc_ref, ksa_ref, vsa_ref, kw_ref, vwa_ref, gt_ref, o_ref, *, tk, T, gb):
    qi = pl.program_id(2)
    nq = Q_BLOCK
    nr = HPG * nq
    nc = kc_ref.shape[2]
    n_sel = nc // CMP_PER_SEL
    va = vsa_ref.shape[1]
    pos_q = qi * nq + lax.broadcasted_iota(jnp.int32, (nq, 1), 0)
    cbias = jnp.where((_cmp_block_of_column(nc) * L_CMP + (L_CMP - 1)) <= pos_q, 0.0, NEG_INF)

    def normalise(r):
        l = r[..., HEAD_DIM:HEAD_DIM + 1]
        return r[..., :HEAD_DIM] * (1.0 / jnp.where(l > 0.0, l, 1.0))

    qs, q_augs, o_cs = [], [], []
    for gl in range(gb):
        qrows = q_ref[gl].astype(F32).T
        q = jnp.concatenate([qrows[:, h * HEAD_DIM:(h + 1) * HEAD_DIM] for h in range(HPG)], axis=0)
        q = q.astype(BF16)
        s_c = jnp.dot(q, kc_ref[gl], preferred_element_type=F32).reshape(HPG, nq, nc)
        e_c, inv_c = _bias_softmax(s_c, cbias[None])
        p_c = e_c * inv_c
        o_cs.append(jnp.dot(p_c.reshape(nr, nc).astype(BF16), vc_ref[gl],
                            preferred_element_type=F32).reshape(HPG, nq, HEAD_DIM))
        imp = p_c[0]
        for h in range(1, HPG):
            imp = imp + p_c[h]
        imp = imp[:, :n_sel] + imp[:, n_sel:]
        sel = _select_blocks(imp, pos_q, n_sel).T
        unsel = jnp.where(sel > 0.5, 0.0, NEG_INF).astype(BF16)
        qs.append(q)
        q_augs.append(jnp.concatenate([jnp.concatenate([unsel] * HPG, axis=0), q], axis=1))

    def sel_tile(gl, k0, carry, causal):
        m, acc = carry
        s = jnp.dot(q_augs[gl], ksa_ref[gl, :, pl.ds(k0, tk)], preferred_element_type=F32)
        s = s.reshape(HPG, nq, tk)
        if causal:
            kpos = k0 + lax.broadcasted_iota(jnp.int32, (nq, tk), 1)
            s = s + jnp.where(kpos <= pos_q, 0.0, NEG_INF)[None]
        m_new = jnp.maximum(m, jnp.max(s, axis=-1, keepdims=True))
        p = jnp.exp(s - m_new).astype(BF16)
        pv = lax.dot_general(p.reshape(nr, tk), vsa_ref[gl, :, pl.ds(k0, tk)], NT_DIMS,
                             preferred_element_type=F32)
        return m_new, jnp.exp(m - m_new) * acc + pv.reshape(HPG, nq, va)

    def sel_tiles(k0, carries, causal):
        return tuple(sel_tile(gl, k0, carries[gl], causal) for gl in range(gb))

    last = (qi * nq + nq + tk - 1) // tk - 1
    init = (jnp.full((HPG, nq, 1), M_FLOOR, F32), jnp.zeros((HPG, nq, va), F32))
    carries = lax.fori_loop(0, last, lambda kt, c: sel_tiles(pl.multiple_of(kt * tk, tk), c, False),
                            (init,) * gb)
    carries = sel_tiles(pl.multiple_of(last * tk, tk), carries, True)

    wl = min(WINDOW + nq, T)
    w0 = pl.multiple_of(jnp.maximum(qi * nq + nq - wl, 0), LANES)
    dpos = pos_q - (w0 + lax.broadcasted_iota(jnp.int32, (nq, wl), 1))
    wbias = jnp.where((dpos >= 0) & (dpos < WINDOW), 0.0, NEG_INF)[None]
    for gl in range(gb):
        o_s = normalise(carries[gl][1])
        s_w = jnp.dot(qs[gl], kw_ref[gl, :, pl.ds(w0, wl)], preferred_element_type=F32).reshape(HPG, nq, wl)
        s_w = s_w + wbias
        e_w = jnp.exp(s_w - jnp.maximum(jnp.max(s_w, axis=-1, keepdims=True), M_FLOOR)).astype(BF16)
        o_w = normalise(lax.dot_general(e_w.reshape(nr, wl), vwa_ref[gl, :, pl.ds(w0, wl)], NT_DIMS,
                                        preferred_element_type=F32).reshape(HPG, nq, va))
        gt = gt_ref[gl]
        o_c = o_cs[gl]
        outs = []
        for h in range(HPG):
            outs.append(gt[:, h:h + 1] * o_c[h] + gt[:, HPG + h:HPG + h + 1] * o_s[h]
                        + gt[:, 2 * HPG + h:2 * HPG + h + 1] * o_w[h])
        o_ref[:, gl * HPG * HEAD_DIM:(gl + 1) * HPG * HEAD_DIM] = jnp.concatenate(outs, axis=1).astype(BF16)


def _block_expansion(n_sel, n_keys):
    return (jnp.arange(n_keys)[None, :] // L_SEL == jnp.arange(n_sel)[:, None]).astype(BF16)


def _cmp_layouts(kc, vc):
    order = jnp.concatenate([jnp.arange(0, kc.shape[1], CMP_PER_SEL), jnp.arange(1, kc.shape[1], CMP_PER_SEL)])
    kct = kc[:, order].transpose(0, 2, 3, 1).astype(BF16)
    vcr = vc[:, order].transpose(0, 2, 1, 3).astype(BF16)
    return kct, vcr


def _attend_prompt(qT, kvbT, gT, kc, vc, *, tk):
    B, _, T = qT.shape
    G = N_KV_HEADS
    nq = Q_BLOCK
    assert T % tk == 0 and T % nq == 0 and tk % L_SEL == 0
    nc = T // L_CMP
    n_sel = T // L_SEL
    kct, vcr = _cmp_layouts(kc, vc)
    q4 = qT.reshape(B, G, HPG * HEAD_DIM, T)
    kv5 = kvbT.reshape(B, 4, G, HEAD_DIM, T)
    gates = gT.reshape(B, 3, G, HPG, T).transpose(0, 2, 4, 1, 3).reshape(B, G, T, 3 * HPG)
    e = jnp.broadcast_to(_block_expansion(n_sel, T)[None, None], (B, G, n_sel, T))
    ks_aug = jnp.concatenate([e, kv5[:, 0]], axis=2)
    ones_pad = jnp.zeros((B, G, BF16_SUBLANES, T), BF16).at[:, :, 0].set(1.0)
    vs_aug = jnp.concatenate([kv5[:, 1], ones_pad], axis=2)
    vw_aug = jnp.concatenate([kv5[:, 3], ones_pad], axis=2)
    va = HEAD_DIM + BF16_SUBLANES
    gb = GROUPS_PER_STEP
    assert G % gb == 0
    per_group = lambda rows: pl.BlockSpec((None, gb, rows, T), lambda b, g, i: (b, g, 0, 0))
    return pl.pallas_call(
        functools.partial(_attn_prompt_kernel, tk=tk, T=T, gb=gb),
        grid=(B, G // gb, T // nq),
        in_specs=[
            pl.BlockSpec((None, gb, HPG * HEAD_DIM, nq), lambda b, g, i: (b, g, 0, i)),
            pl.BlockSpec((None, gb, HEAD_DIM, nc), lambda b, g, i: (b, g, 0, 0)),
            pl.BlockSpec((None, gb, nc, HEAD_DIM), lambda b, g, i: (b, g, 0, 0)),
            per_group(n_sel + HEAD_DIM), per_group(va),
            pl.BlockSpec((None, None, gb, HEAD_DIM, T), lambda b, g, i: (b, 2, g, 0, 0)),
            per_group(va),
            pl.BlockSpec((None, gb, nq, 3 * HPG), lambda b, g, i: (b, g, i, 0)),
        ],
        out_specs=pl.BlockSpec((None, nq, gb * HPG * HEAD_DIM), lambda b, g, i: (b, i, g)),
        out_shape=jax.ShapeDtypeStruct((B, T, D_Q), BF16),
        compiler_params=_cparams(("arbitrary", "arbitrary", "arbitrary")),
        name="nsa_attend_prompt",
    )(q4, kct, vcr, ks_aug, vs_aug, kv5, vw_aug, gates)


PAGE_LOOP_UNROLL = 4


def _attn_sample_kernel(pt_ref, q_ref, kc_ref, vc_ref, kwc_ref, vwc_ref, kwn_ref, vwn_ref, ksn_ref, vsn_ref,
                        gt_ref, e_ref, dm_ref, ks_hbm, vs_hbm, o_ref,
                        kbuf, vbuf, ksem, vsem, s_s, sel_s, m_s, l_s, acc_s, oc_s, ow_s,
                        *, nchunks, npc, total, past, tq):
    c = pl.program_id(1)
    kslot = _stream_pages(pt_ref, ks_hbm, kbuf, ksem, nchunks=nchunks, npc=npc, total=total)
    vslot = _stream_pages(pt_ref, vs_hbm, vbuf, vsem, nchunks=nchunks, npc=npc, total=total)
    q = q_ref[...]
    nr = q.shape[0]
    ngt = N_KV_HEADS * tq
    pos_q = past + lax.broadcasted_iota(jnp.int32, (nr, 1), 0) % tq
    n_sel = sel_s.shape[1]
    sel_past = past // L_SEL

    def fold(o_full):
        o = o_full * dm_ref[...]
        out = o[:, 0:HEAD_DIM]
        for gi in range(1, N_KV_HEADS):
            out = out + o[:, gi * HEAD_DIM:(gi + 1) * HEAD_DIM]
        return out

    def nt(p, vt):
        return lax.dot_general(p.astype(BF16), vt.astype(BF16), NT_DIMS, preferred_element_type=F32)

    @pl.when(c == 0)
    def _():
        nc = kc_ref.shape[1]
        s_c = jnp.dot(q, kc_ref[...], preferred_element_type=F32)
        cmask = (_cmp_block_of_column(nc) * L_CMP + (L_CMP - 1)) <= pos_q
        p_c = _masked_softmax(s_c, cmask)
        oc_s[...] = fold(nt(p_c, vc_ref[...]))
        imp = p_c[0:ngt]
        for h in range(1, HPG):
            imp = imp + p_c[h * ngt:(h + 1) * ngt]
        imp = imp[:, :n_sel] + imp[:, n_sel:]
        pad = LANES - ngt
        imp = jnp.concatenate([imp, jnp.zeros((pad, n_sel), F32)], axis=0)
        pos_pad = past + lax.broadcasted_iota(jnp.int32, (LANES, 1), 0) % tq
        sel = _select_blocks(imp, pos_pad, n_sel).T[0:ngt]
        sel_s[...] = jnp.concatenate([sel] * HPG, axis=0)

        nwc = kwc_ref.shape[1]
        nwn = kwn_ref.shape[1]
        s_w = jnp.concatenate([jnp.dot(q, kwc_ref[...].astype(BF16), preferred_element_type=F32),
                               jnp.dot(q, kwn_ref[...].astype(BF16), preferred_element_type=F32)], axis=1)
        lane = lax.broadcasted_iota(jnp.int32, (nr, nwc + nwn), 1)
        pos_w = jnp.where(lane < nwc, past - nwc + lane, past + lane - nwc)
        dpos = pos_q - pos_w
        p_w = _masked_softmax(s_w, (dpos >= 0) & (dpos < WINDOW) & (pos_w >= 0))
        ow_s[...] = fold(nt(p_w[:, :nwc], vwc_ref[...]) + nt(p_w[:, nwc:], vwn_ref[...]))

        nsn = ksn_ref.shape[1]
        s_n = jnp.dot(q, ksn_ref[...].astype(BF16), preferred_element_type=F32)
        kpos = past + lax.broadcasted_iota(jnp.int32, (nr, nsn), 1)
        msk = (sel_s[:, sel_past:sel_past + 1] > 0.5) & (kpos <= pos_q)
        s_n = jnp.where(msk, s_n, NEG_INF)
        m0 = jnp.max(s_n, axis=-1, keepdims=True)
        p_n = jnp.where(msk, jnp.exp(s_n - m0), 0.0)
        m_s[...] = m0
        l_s[...] = jnp.sum(p_n, axis=-1, keepdims=True)
        acc_s[...] = nt(p_n, vsn_ref[...])

    def qk(p, carry):
        r0 = pl.multiple_of(p * PAGE_ROWS, PAGE_ROWS)
        c0 = pl.multiple_of(p * PAGE_SIZE, PAGE_SIZE)
        s_s[:, pl.ds(c0, PAGE_SIZE)] = jnp.dot(q, kbuf[kslot, pl.ds(r0, PAGE_ROWS), :].astype(BF16),
                                                preferred_element_type=F32)
        return carry

    lax.fori_loop(0, npc, qk, 0, unroll=PAGE_LOOP_UNROLL)
    nk = npc * PAGE_SIZE
    sel_c = sel_s[:, pl.ds(pl.multiple_of(c * (nk // L_SEL), LANES), nk // L_SEL)]
    chosen = jnp.dot(sel_c.astype(BF16), e_ref[...], preferred_element_type=F32)
    kpos = c * nk + lax.broadcasted_iota(jnp.int32, (nr, nk), 1)
    msk = (chosen > 0.5) & (kpos <= pos_q)
    s = jnp.where(msk, s_s[...], NEG_INF)
    m_old = m_s[...]
    m_new = jnp.maximum(m_old, jnp.max(s, axis=-1, keepdims=True))
    alpha = jnp.exp(m_old - m_new)
    p = jnp.where(msk, jnp.exp(s - m_new), 0.0)
    l_s[...] = alpha * l_s[...] + jnp.sum(p, axis=-1, keepdims=True)
    m_s[...] = m_new
    s_s[...] = p

    def pv(p_, acc):
        r0 = pl.multiple_of(p_ * PAGE_ROWS, PAGE_ROWS)
        c0 = pl.multiple_of(p_ * PAGE_SIZE, PAGE_SIZE)
        return acc + nt(s_s[:, pl.ds(c0, PAGE_SIZE)], vbuf[vslot, pl.ds(r0, PAGE_ROWS), :])

    acc = lax.fori_loop(0, npc, pv, jnp.zeros((nr, PAGE_ROWS), F32), unroll=PAGE_LOOP_UNROLL)
    acc_s[...] = alpha * acc_s[...] + acc

    @pl.when(c == nchunks - 1)
    def _():
        l = l_s[...]
        o_sel = fold(acc_s[...]) / jnp.where(l > 0.0, l, 1.0)
        gt = gt_ref[...]
        o_ref[...] = gt[:, 0:1] * oc_s[...] + gt[:, 1:2] * o_sel + gt[:, 2:3] * ow_s[...]


def _attend_sample(qT, kvT, gT, kc_all, vc_all, win_k, win_v, pool_ks, pool_vs, page_table, *, npc):
    Bs, NPG = page_table.shape
    tq = qT.shape[1] // Bs
    G = N_KV_HEADS
    past = NPG * PAGE_SIZE
    nchunks = NPG // npc
    nk = npc * PAGE_SIZE
    assert NPG % npc == 0 and nk // L_SEL == LANES
    nr = HPG * G * tq
    eye = jnp.eye(G, dtype=qT.dtype)
    q5 = qT.reshape(G, HPG, HEAD_DIM, tq, Bs).transpose(4, 1, 0, 3, 2)
    q_bd = jnp.einsum('bhgtd,gk->bhgtkd', q5, eye).reshape(Bs, nr, G * HEAD_DIM)
    dm = jnp.broadcast_to(jnp.eye(G, dtype=F32)[None, :, None, :, None],
                          (HPG, G, tq, G, HEAD_DIM)).reshape(nr, G * HEAD_DIM)
    gates = gT.reshape(3, G, HPG, tq, Bs).transpose(4, 2, 1, 3, 0).reshape(Bs, nr, 3)

    def new_rows(kind):
        r = kvT[kind].reshape(G * HEAD_DIM, tq, Bs).transpose(2, 0, 1)
        return jnp.pad(r, ((0, 0), (0, 0), (0, LANES - tq)))

    nc = kc_all.shape[1]
    half = nc // CMP_PER_SEL
    n_sel = -(-(half) // LANES) * LANES
    assert past // L_SEL < n_sel

    def cmp_fm(x):
        padw = ((0, 0), (0, 0), (0, n_sel - half))
        parts = [jnp.pad(x[:, par::CMP_PER_SEL].transpose(0, 2, 3, 1).reshape(Bs, G * HEAD_DIM, half), padw)
                 for par in range(CMP_PER_SEL)]
        return jnp.concatenate(parts, axis=2).astype(BF16)

    n_win = win_k.shape[-1]
    wk = win_k.reshape(Bs, G * HEAD_DIM, n_win)
    wv = win_v.reshape(Bs, G * HEAD_DIM, n_win)
    e = _block_expansion(nk // L_SEL, nk)
    ks2 = pool_ks.reshape(pool_ks.shape[0], PAGE_ROWS, PAGE_SIZE)
    vs2 = pool_vs.reshape(pool_vs.shape[0], PAGE_ROWS, PAGE_SIZE)
    per_b = lambda shape: pl.BlockSpec((None,) + shape, lambda b, c, pt: (b, 0, 0))
    kern = functools.partial(_attn_sample_kernel, nchunks=nchunks, npc=npc, total=Bs * nchunks, past=past, tq=tq)
    o = pl.pallas_call(
        kern,
        grid_spec=pltpu.PrefetchScalarGridSpec(
            num_scalar_prefetch=1,
            grid=(Bs, nchunks),
            in_specs=[
                per_b((nr, G * HEAD_DIM)),
                per_b((G * HEAD_DIM, 2 * n_sel)), per_b((G * HEAD_DIM, 2 * n_sel)),
                per_b((G * HEAD_DIM, n_win)), per_b((G * HEAD_DIM, n_win)),
                per_b((G * HEAD_DIM, LANES)), per_b((G * HEAD_DIM, LANES)),
                per_b((G * HEAD_DIM, LANES)), per_b((G * HEAD_DIM, LANES)),
                per_b((nr, 3)),
                pl.BlockSpec(e.shape, lambda b, c, pt: (0, 0)),
                pl.BlockSpec(dm.shape, lambda b, c, pt: (0, 0)),
                pl.BlockSpec(memory_space=pl.ANY),
                pl.BlockSpec(memory_space=pl.ANY),
            ],
            out_specs=pl.BlockSpec((None, nr, HEAD_DIM), lambda b, c, pt: (b, 0, 0)),
            scratch_shapes=[
                pltpu.VMEM((2, npc * PAGE_ROWS, PAGE_SIZE), F32), pltpu.VMEM((2, npc * PAGE_ROWS, PAGE_SIZE), F32),
                pltpu.SemaphoreType.DMA((2,)), pltpu.SemaphoreType.DMA((2,)),
                pltpu.VMEM((nr, nk), F32), pltpu.VMEM((nr, n_sel), F32),
                pltpu.VMEM((nr, 1), F32), pltpu.VMEM((nr, 1), F32), pltpu.VMEM((nr, G * HEAD_DIM), F32),
                pltpu.VMEM((nr, HEAD_DIM), F32), pltpu.VMEM((nr, HEAD_DIM), F32),
            ],
        ),
        out_shape=jax.ShapeDtypeStruct((Bs, nr, HEAD_DIM), F32),
        compiler_params=_cparams(("arbitrary", "arbitrary")),
        name="nsa_attend_sample",
    )(page_table, q_bd, cmp_fm(kc_all), cmp_fm(vc_all), wk, wv, new_rows(4), new_rows(5), new_rows(2), new_rows(3),
      gates, e, dm, ks2, vs2)
    o = o.reshape(Bs, HPG, G, tq, HEAD_DIM).transpose(3, 0, 2, 1, 4)
    return o.reshape(tq * Bs, D_Q).astype(BF16)


def _out_proj_kernel(a_ref, w_ref, x_ref, o_ref):
    o_ref[...] = x_ref[...] + jnp.dot(a_ref[...], w_ref[...], preferred_element_type=F32)


def _out_proj(a, w, x, *, tm):
    R, Kd = a.shape
    N = w.shape[1]
    assert R % tm == 0
    return pl.pallas_call(
        _out_proj_kernel,
        grid=(R // tm,),
        in_specs=[pl.BlockSpec((tm, Kd), lambda i: (i, 0)), pl.BlockSpec((Kd, N), lambda i: (0, 0)),
                  pl.BlockSpec((tm, N), lambda i: (i, 0))],
        out_specs=pl.BlockSpec((tm, N), lambda i: (i, 0)),
        out_shape=jax.ShapeDtypeStruct((R, N), F32),
        compiler_params=_cparams(("arbitrary",)),
        name="out_proj",
    )(a, w.astype(BF16), x)


PROMPT_ROW_TILE = 512
S5_CHUNK = 256
SEL_KEY_TILE = 512
SAMPLE_PAGES_PER_CHUNK = 64


def _feature_major(cache):
    nd = cache.ndim
    return jnp.moveaxis(cache, nd - 3, nd - 1)


def _nsa_prompt_layer(x, g, w_in, q_gain, k_gain, ck, cv, w_out):
    B, T, D = x.shape
    G = N_KV_HEADS
    qT, kvT, kvbT, gT, kcp, vcp = _nsa_project(x, jnp.arange(T), g, w_in, q_gain, k_gain,
                                               tm=PROMPT_ROW_TILE, paged=True)
    npg = T // PAGE_SIZE
    ptab = jnp.arange(B * npg, dtype=jnp.int32).reshape(B, npg)
    kc = _compress(kcp, ptab, _compress_tables(*ck), npc=npg)
    vc = _compress(vcp, ptab, _compress_tables(*cv), npc=npg)
    o = _attend_prompt(qT, kvbT, gT, kc, vc, tk=SEL_KEY_TILE)
    y = _out_proj(o.reshape(B * T, D_Q), w_out, x.reshape(B * T, D), tm=PROMPT_ROW_TILE).reshape(B, T, D)
    rows = [kvT[:, k].reshape(B, G, HEAD_DIM, T).transpose(0, 3, 1, 2) for k in range(N_KINDS)]
    n_keep = min(WINDOW, T)
    rows[4] = rows[4][:, T - n_keep:]
    rows[5] = rows[5][:, T - n_keep:]
    return y, rows


def _nsa_sample_layer(x_tm, Bs, pools, win_k, win_v, page_table, g, w_in, q_gain, k_gain, ck, cv, w_out):
    _, R, D = x_tm.shape
    G = N_KV_HEADS
    tq = R // Bs
    npg = page_table.shape[1]
    past = npg * PAGE_SIZE
    pos = past + jnp.arange(R) // Bs
    qT, kvT, _, gT = _nsa_project(x_tm, pos, g, w_in, q_gain, k_gain, tm=R, paged=False)
    qT, kvT, gT = qT[0], kvT[0], gT[0]
    new_pad = -(-tq // L_SEL) * L_SEL
    assert new_pad <= PAGE_SIZE
    new_tab = jnp.arange(Bs, dtype=jnp.int32).reshape(1, Bs)

    def cmp_all(pool, kind, tabs):
        past_blocks = _compress(_feature_major(pool), page_table, tabs, npc=SAMPLE_PAGES_PER_CHUNK)
        page = kvT[kind].reshape(G, HEAD_DIM, tq, Bs).transpose(3, 0, 1, 2)
        page = jnp.pad(page, ((0, 0), (0, 0), (0, 0), (0, PAGE_SIZE - tq)))
        new_blocks = _compress(page, new_tab, tabs, npc=Bs).reshape(Bs, CMP_PER_PAGE, G, HEAD_DIM)
        return jnp.concatenate([past_blocks, new_blocks[:, :new_pad // L_CMP]], axis=1)

    kc_all = cmp_all(pools[0], 0, _compress_tables(*ck))
    vc_all = cmp_all(pools[1], 1, _compress_tables(*cv))
    o = _attend_sample(qT, kvT, gT, kc_all, vc_all, _feature_major(win_k), _feature_major(win_v),
                       _feature_major(pools[2]), _feature_major(pools[3]), page_table,
                       npc=SAMPLE_PAGES_PER_CHUNK)
    y = _out_proj(o, w_out, x_tm[0], tm=R)[None]
    rows = [kvT[k].reshape(G, HEAD_DIM, tq, Bs).transpose(3, 2, 0, 1) for k in range(N_KINDS)]
    return y, rows


def kernel(x_prompt, x_sample, cache_k_cmp, cache_v_cmp, cache_k_slc, cache_v_slc, cache_k_win, cache_v_win,
           state_ssm_re, state_ssm_im, state_conv, page_table, norm_mix, norm_ffn,
           ssm_w_in, ssm_a_re, ssm_a_im, ssm_log_dt, ssm_b_re, ssm_b_im, ssm_c_re, ssm_c_im,
           ssm_d, ssm_w_gate, ssm_w_out,
           nsa_w_in, nsa_q_gain, nsa_k_gain, nsa_ck_w1, nsa_ck_b1, nsa_ck_w2,
           nsa_cv_w1, nsa_cv_b1, nsa_cv_w2, nsa_w_out,
           ffn_w_up, ffn_conv_w, ffn_conv_b, ffn_w_down):
    B, T, D = x_prompt.shape
    Bs, Tq, _ = x_sample.shape
    depth = norm_mix.shape[0]
    Fd = ffn_w_down.shape[1]
    G, P = ssm_a_re.shape[1:]
    xp = x_prompt
    xs = x_sample.transpose(1, 0, 2).reshape(1, Tq * Bs, D)
    p_sre, p_sim, s_sre, s_sim, p_cv, s_cv = [], [], [], [], [], []
    p_rows = [[] for _ in range(N_KINDS)]
    s_rows = [[] for _ in range(N_KINDS)]
    for i in range(depth):
        li = i // 2
        if i % 2 == 0:
            tabs = _s5_tables(ssm_a_re[li], ssm_a_im[li], ssm_log_dt[li], ssm_b_re[li], ssm_b_im[li],
                              ssm_c_re[li], ssm_c_im[li])
            sp = (norm_mix[i], ssm_w_in[li], tabs, ssm_d[li], ssm_w_gate[li], ssm_w_out[li])
            z = jnp.zeros((B, SUBLANES, G * P), F32)
            xp, fr, fi = _s5_layer(xp, z, z, *sp, tc=min(S5_CHUNK, T), slab=1)
            p_sre.append(fr[:, SUBLANES - 1].reshape(B, G, P))
            p_sim.append(fi[:, SUBLANES - 1].reshape(B, G, P))
            xs, fr, fi = _s5_layer(xs, state_ssm_re[li].reshape(1, Bs, G * P).astype(F32),
                                   state_ssm_im[li].reshape(1, Bs, G * P).astype(F32), *sp, tc=Tq * Bs, slab=Bs)
            s_sre.append(fr.reshape(Bs, G, P))
            s_sim.append(fi.reshape(Bs, G, P))
        else:
            ap = (norm_mix[i], nsa_w_in[li], nsa_q_gain[li], nsa_k_gain[li],
                  (nsa_ck_w1[li], nsa_ck_b1[li], nsa_ck_w2[li]), (nsa_cv_w1[li], nsa_cv_b1[li], nsa_cv_w2[li]),
                  nsa_w_out[li])
            xp, rp = _nsa_prompt_layer(xp, *ap)
            xs, rs = _nsa_sample_layer(xs, Bs, (cache_k_cmp[li], cache_v_cmp[li], cache_k_slc[li], cache_v_slc[li]),
                                       cache_k_win[li], cache_v_win[li], page_table, *ap)
            for j in range(N_KINDS):
                p_rows[j].append(rp[j])
                s_rows[j].append(rs[j])
        fw = (norm_ffn[i], ffn_w_up[i], ffn_conv_w[i], ffn_conv_b[i], ffn_w_down[i])
        xp, cp = _conv_ffn(xp, jnp.zeros((B, SUBLANES, Fd), F32), *fw, shift=1, tm=min(PROMPT_ROW_TILE, T))
        p_cv.append(cp[:, SUBLANES - (CONV_W - 1):])
        init = state_conv[i].astype(F32).transpose(1, 0, 2).reshape(1, (CONV_W - 1) * Bs, Fd)
        xs, cs = _conv_ffn(xs, init, *fw, shift=Bs, tm=Tq * Bs)
        s_cv.append(cs.reshape(CONV_W - 1, Bs, Fd).transpose(1, 0, 2))
    st = jnp.stack
    y_sample = xs.reshape(Tq, Bs, D).transpose(1, 0, 2)
    return (xp, y_sample,
            st(p_sre), st(p_sim), *[st(r) for r in p_rows], st(p_cv),
            st(s_sre), st(s_sim), *[st(r) for r in s_rows], st(s_cv))
```

```python
import functools
import math

import jax
import jax.numpy as jnp
from jax import lax
from jax.experimental import pallas as pl
from jax.experimental.pallas import tpu as pltpu

F32 = jnp.float32
BF16 = jnp.bfloat16

NORM_EPS = 1e-6
SSM_CH = 16
SSM_STATE = 64
N_HEADS = 16
N_KV_HEADS = 4
HEAD_DIM = 64
L_CMP = 32
L_SEL = 64
TOP_N = 16
WINDOW = 512
PAGE_SIZE = 128
ROPE_THETA = 10000.0
NEG_INF = -1e30
FORCE_SCORE = 1e4
CONV_W = 3

LANES = 128
SUBLANES = 8
VMEM_LIMIT = 56 * 1024 * 1024


def _cparams(sem):
    return pltpu.CompilerParams(dimension_semantics=sem, vmem_limit_bytes=VMEM_LIMIT)


def _rms(x, g):
    return x * lax.rsqrt(jnp.mean(x * x, axis=-1, keepdims=True) + NORM_EPS) * g


FFN_MIN_SUBTILE = 256


def _ffn_kernel(x_ref, g_ref, wa_ref, wb_ref, cw_ref, cb_ref, wd_ref, init_ref,
                o_ref, buf_ref, xn_s, acc_s, aext_s, carry_s, *, shift, base, tm, nf, nsub):
    i = pl.program_id(1)
    j = pl.program_id(2)

    @pl.when(j == 0)
    def _():
        x = x_ref[...]
        xn_s[...] = _rms(x, g_ref[...]).astype(BF16)
        acc_s[...] = x

    @pl.when(i == 0)
    def _():
        aext_s[0:base, :] = init_ref[...]

    @pl.when(i > 0)
    def _():
        aext_s[0:base, :] = carry_s[j]

    cw = cw_ref[...]
    ts = tm // nsub
    for sub in range(nsub):
        r0 = sub * ts
        xn = xn_s[r0:r0 + ts, :]
        a = jnp.dot(xn, wa_ref[...], preferred_element_type=F32)
        b = jnp.dot(xn, wb_ref[...], preferred_element_type=F32)
        aext_s[base + r0:base + r0 + ts, :] = a
        a1 = aext_s[base - shift + r0:base - shift + r0 + ts, :]
        a2 = aext_s[base - 2 * shift + r0:base - 2 * shift + r0 + ts, :]
        c = cb_ref[...] + cw[0:1, :] * a2
        c = c + cw[1:2, :] * a1
        c = c + cw[2:3, :] * a
        h = (jax.nn.silu(c) * b).astype(BF16)
        acc_s[r0:r0 + ts, :] += jnp.dot(h, wd_ref[...], preferred_element_type=F32)
    tail = aext_s[tm:tm + base, :]
    carry_s[j] = tail
    buf_ref[...] = tail

    @pl.when(j == nf - 1)
    def _():
        o_ref[...] = acc_s[...]


def _conv_ffn(x, init, g, w_up, conv_w, conv_b, w_down, *, shift, tm):
    S, R, D = x.shape
    Fd = w_down.shape[0]
    base = init.shape[1]
    nf = 2
    tf = Fd // nf
    assert tf * nf == Fd and tf % LANES == 0 and R % tm == 0 and base >= 2 * shift
    wu = w_up.astype(BF16)
    wd = w_down.astype(BF16)
    nsub = 2 if tm % (2 * FFN_MIN_SUBTILE) == 0 else 1
    kern = functools.partial(_ffn_kernel, shift=shift, base=base, tm=tm, nf=nf, nsub=nsub)
    out, buf = pl.pallas_call(
        kern,
        grid=(S, R // tm, nf),
        in_specs=[
            pl.BlockSpec((None, tm, D), lambda s, i, j: (s, i, 0)),
            pl.BlockSpec((1, D), lambda s, i, j: (0, 0)),
            pl.BlockSpec((D, tf), lambda s, i, j: (0, j)),
            pl.BlockSpec((D, tf), lambda s, i, j: (0, nf + j)),
            pl.BlockSpec((CONV_W, tf), lambda s, i, j: (0, j)),
            pl.BlockSpec((1, tf), lambda s, i, j: (0, j)),
            pl.BlockSpec((tf, D), lambda s, i, j: (j, 0)),
            pl.BlockSpec((None, base, tf), lambda s, i, j: (s, 0, j)),
        ],
        out_specs=[
            pl.BlockSpec((None, tm, D), lambda s, i, j: (s, i, 0)),
            pl.BlockSpec((None, None, base, tf), lambda s, i, j: (s, i, 0, j)),
        ],
        out_shape=[jax.ShapeDtypeStruct((S, R, D), F32), jax.ShapeDtypeStruct((S, R // tm, base, Fd), F32)],
        scratch_shapes=[
            pltpu.VMEM((tm, D), BF16),
            pltpu.VMEM((tm, D), F32),
            pltpu.VMEM((tm + base, tf), F32),
            pltpu.VMEM((nf, base, tf), F32),
        ],
        compiler_params=_cparams(("arbitrary", "arbitrary", "arbitrary")),
        name="conv_ffn",
    )(x, g.reshape(1, D), wu, wu, conv_w, conv_b.reshape(1, Fd), wd, init)
    return out, buf[:, R // tm - 1]


def _s5_tables(a_re, a_im, log_dt, b_re, b_im, c_re, c_im):
    G, P = a_re.shape
    nt = G * SSM_CH // LANES
    gt = G // nt
    a_re, a_im = a_re.astype(F32), a_im.astype(F32)
    dt = jnp.exp(log_dt.astype(F32))[:, None]
    mag = jnp.exp(a_re * dt)
    lam_re, lam_im = mag * jnp.cos(a_im * dt), mag * jnp.sin(a_im * dt)
    den = a_re * a_re + a_im * a_im
    n_re, n_im = lam_re - 1.0, lam_im
    coef_re = (n_re * a_re + n_im * a_im) / den
    coef_im = (n_im * a_re - n_re * a_im) / den
    b_re, b_im = b_re.astype(F32), b_im.astype(F32)
    bb_re = coef_re[..., None] * b_re - coef_im[..., None] * b_im
    bb_im = coef_re[..., None] * b_im + coef_im[..., None] * b_re
    eye = jnp.eye(gt, dtype=F32)

    def in_proj(bb):
        v = bb.reshape(nt, gt, P, SSM_CH).transpose(0, 1, 3, 2)
        return jnp.einsum('igcp,gh->igchp', v, eye).reshape(nt, gt * SSM_CH, gt * P)

    def out_proj(c):
        v = c.astype(F32).reshape(nt, gt, SSM_CH, P)
        return jnp.einsum('igcp,gh->igphc', v, eye).reshape(nt, gt * P, gt * SSM_CH)

    def cmul(xr, xi, yr, yi):
        return xr * yr - xi * yi, xr * yi + xi * yr

    pows = [(jnp.ones_like(lam_re), jnp.zeros_like(lam_im))]
    for _ in range(SUBLANES):
        pows.append(cmul(*pows[-1], lam_re, lam_im))

    def in_proj_pair(j):
        pr, pi = pows[j][0][..., None], pows[j][1][..., None]
        return jnp.concatenate([in_proj(pr * bb_re - pi * bb_im), in_proj(pr * bb_im + pi * bb_re)], axis=2)

    wb = in_proj_pair(0).astype(BF16)
    wb_fir = jnp.concatenate([in_proj_pair(j) for j in range(SUBLANES)], axis=1).astype(BF16)
    wc = jnp.concatenate([out_proj(c_re), -out_proj(c_im)], axis=1).astype(BF16)
    flat = lambda x: x.reshape(1, G * P)
    rep = lambda x: jnp.broadcast_to(flat(x), (SUBLANES, G * P))
    lam = jnp.concatenate([flat(pows[1][0]), flat(pows[1][1]), jnp.zeros((SUBLANES - 2, G * P), F32)], axis=0)
    lp_re = jnp.concatenate([flat(p[0]) for p in pows[1:]], axis=0)
    lp_im = jnp.concatenate([flat(p[1]) for p in pows[1:]], axis=0)
    return dict(wb=wb, wb_fir=wb_fir, wc=wc, lam=lam, lp_re=lp_re, lp_im=lp_im,
                l8_re=rep(pows[SUBLANES][0]), l8_im=rep(pows[SUBLANES][1]))


def _s5_kernel(x_ref, g_ref, win_ref, wb_ref, wc_ref, lam_ref, lpr_ref, lpi_ref, l8r_ref, l8i_ref, d_ref, wg_ref,
               wo_ref, ir_ref, ii_ref, o_ref, fr_ref, fi_ref, sre, sim, uext, y_s, *, tc, slab, nt):
    ci = pl.program_id(1)
    hist = SUBLANES
    x = x_ref[...]
    xn = _rms(x, g_ref[...]).astype(BF16)
    uext[hist:hist + tc, :] = jnp.dot(xn, win_ref[...], preferred_element_type=F32)
    n2 = sre.shape[1] // nt

    if slab == 1:
        @pl.when(ci == 0)
        def _():
            uext[0:hist, :] = jnp.zeros((hist, uext.shape[1]), F32)
            fr_ref[...] = ir_ref[...]
            fi_ref[...] = ii_ref[...]

        for i in range(nt):
            ls = slice(i * LANES, (i + 1) * LANES)
            taps = jnp.concatenate([uext[hist - j:hist - j + tc, ls].astype(BF16) for j in range(SUBLANES)], axis=1)
            v = jnp.dot(taps, wb_ref[i], preferred_element_type=F32)
            sre[:, i * n2:(i + 1) * n2] = v[:, :n2]
            sim[:, i * n2:(i + 1) * n2] = v[:, n2:]
        uext[0:hist, :] = uext[tc:tc + hist, :]

        first = ci == 0
        nh = 2
        wh = sre.shape[1] // nh
        for hh in range(nh):
            cs = slice(hh * wh, (hh + 1) * wh)
            l8r, l8i = l8r_ref[:, cs], l8i_ref[:, cs]
            pr, pi = fr_ref[:, cs], fi_ref[:, cs]
            h0r, h0i = pr[SUBLANES - 1:SUBLANES, :], pi[SUBLANES - 1:SUBLANES, :]
            lpr, lpi = lpr_ref[:, cs], lpi_ref[:, cs]
            hr = jnp.where(first, lpr * h0r - lpi * h0i, l8r * pr - l8i * pi) + sre[0:SUBLANES, cs]
            hi = jnp.where(first, lpr * h0i + lpi * h0r, l8r * pi + l8i * pr) + sim[0:SUBLANES, cs]
            sre[0:SUBLANES, cs] = hr
            sim[0:SUBLANES, cs] = hi

            def body(k, carry, cs=cs, l8r=l8r, l8i=l8i):
                cr, cim = carry
                r0 = pl.multiple_of(k * SUBLANES, SUBLANES)
                hr = l8r * cr - l8i * cim + sre[pl.ds(r0, SUBLANES), cs]
                hi = l8r * cim + l8i * cr + sim[pl.ds(r0, SUBLANES), cs]
                sre[pl.ds(r0, SUBLANES), cs] = hr
                sim[pl.ds(r0, SUBLANES), cs] = hi
                return hr, hi

            lax.fori_loop(1, tc // SUBLANES, body, (hr, hi), unroll=2)
        fr_ref[...] = sre[tc - SUBLANES:tc, :]
        fi_ref[...] = sim[tc - SUBLANES:tc, :]
    else:
        ub = uext[hist:hist + tc, :].astype(BF16)
        for i in range(nt):
            bu = jnp.dot(ub[:, i * LANES:(i + 1) * LANES], wb_ref[i], preferred_element_type=F32)
            sre[:, i * n2:(i + 1) * n2] = bu[:, :n2]
            sim[:, i * n2:(i + 1) * n2] = bu[:, n2:]
        lr, li = lam_ref[0:1, :], lam_ref[1:2, :]
        hr, hi = ir_ref[...], ii_ref[...]
        for t in range(tc // slab):
            rs = slice(t * slab, (t + 1) * slab)
            hr, hi = lr * hr - li * hi + sre[rs, :], lr * hi + li * hr + sim[rs, :]
            sre[rs, :] = hr
            sim[rs, :] = hi
        fr_ref[...] = hr
        fi_ref[...] = hi

    for i in range(nt):
        cs = slice(i * n2, (i + 1) * n2)
        s_cat = jnp.concatenate([sre[:, cs].astype(BF16), sim[:, cs].astype(BF16)], axis=1)
        y_s[:, i * LANES:(i + 1) * LANES] = jnp.dot(s_cat, wc_ref[i], preferred_element_type=F32)
    z = jax.nn.gelu(y_s[...] + d_ref[...] * uext[hist:hist + tc, :])
    gate = jnp.dot(z.astype(BF16), wg_ref[...], preferred_element_type=F32)
    out = jnp.dot((z * jax.nn.sigmoid(gate)).astype(BF16), wo_ref[...], preferred_element_type=F32)
    o_ref[...] = x_ref[...] + out


def _s5_layer(x, init_re, init_im, g, w_in, tables, d, w_gate, w_out, *, tc, slab):
    S, R, D = x.shape
    wb = tables["wb_fir"] if slab == 1 else tables["wb"]
    wc = tables["wc"]
    nt = wb.shape[0]
    N = tables["lam"].shape[1]
    hb = init_re.shape[1]
    assert R % tc == 0 and tc % SUBLANES == 0 and (slab == 1 or R == tc)
    kern = functools.partial(_s5_kernel, tc=tc, slab=slab, nt=nt)
    const = lambda shape: pl.BlockSpec(shape, lambda s, c: (0,) * len(shape), pipeline_mode=pl.Buffered(1))
    out, fr, fi = pl.pallas_call(
        kern,
        grid=(S, R // tc),
        in_specs=[
            pl.BlockSpec((None, tc, D), lambda s, c: (s, c, 0)),
            const((1, D)),
            const((D, D)),
            const(wb.shape),
            const(wc.shape),
            const((SUBLANES, N)), const((SUBLANES, N)), const((SUBLANES, N)), const((SUBLANES, N)),
            const((SUBLANES, N)),
            const((1, D)),
            const((D, D)),
            const((D, D)),
            pl.BlockSpec((None, hb, N), lambda s, c: (s, 0, 0)),
            pl.BlockSpec((None, hb, N), lambda s, c: (s, 0, 0)),
        ],
        out_specs=[
            pl.BlockSpec((None, tc, D), lambda s, c: (s, c, 0)),
            pl.BlockSpec((None, hb, N), lambda s, c: (s, 0, 0)),
            pl.BlockSpec((None, hb, N), lambda s, c: (s, 0, 0)),
        ],
        out_shape=[jax.ShapeDtypeStruct((S, R, D), F32), jax.ShapeDtypeStruct((S, hb, N), F32),
                   jax.ShapeDtypeStruct((S, hb, N), F32)],
        scratch_shapes=[pltpu.VMEM((tc, N), F32), pltpu.VMEM((tc, N), F32),
                        pltpu.VMEM((SUBLANES + tc, D), F32), pltpu.VMEM((tc, D), F32)],
        compiler_params=_cparams(("arbitrary", "arbitrary")),
        name="s5_layer",
    )(x, g.reshape(1, D), w_in.astype(BF16), wb, wc, tables["lam"], tables["lp_re"], tables["lp_im"],
      tables["l8_re"], tables["l8_im"], d.reshape(1, D).astype(F32),
      w_gate.astype(BF16), w_out.astype(BF16), init_re, init_im)
    return out, fr, fi


D_Q = N_HEADS * HEAD_DIM
KV_DIM = N_KV_HEADS * HEAD_DIM
N_KINDS = 6
N_GATES = 3 * N_HEADS
HALF = HEAD_DIM // 2


def _rope_tables(pos):
    inv = jnp.power(ROPE_THETA, -jnp.arange(HALF, dtype=F32) / HALF)
    ang = inv[:, None] * pos.astype(F32)[None, :]
    return jnp.cos(ang), jnp.sin(ang)


def _nsa_proj_kernel(x_ref, g_ref, wt_ref, qg_ref, kg_ref, cos_ref, sin_ref,
                     q_ref, kv_ref, kvb_ref, gt_ref, *rest, tm, paged):
    if paged:
        kcp_ref, vcp_ref, pt_s = rest
    else:
        (pt_s,) = rest
    xn = _rms(x_ref[...], g_ref[...]).astype(BF16)
    pt_s[...] = lax.dot_general(wt_ref[...], xn, (((1,), (1,)), ((), ())), preferred_element_type=F32)
    cos, sin = cos_ref[...], sin_ref[...]

    def norm_rope(blk, gain):
        y = blk * lax.rsqrt(jnp.mean(blk * blk, axis=0, keepdims=True) + NORM_EPS) * gain
        x1, x2 = y[:HALF], y[HALF:]
        return x1 * cos - x2 * sin, x2 * cos + x1 * sin

    scale = HEAD_DIM ** -0.5
    for h in range(N_HEADS):
        r0 = h * HEAD_DIM
        r1, r2 = norm_rope(pt_s[r0:r0 + HEAD_DIM, :], qg_ref[...])
        q_ref[r0:r0 + HALF, :] = (r1 * scale).astype(BF16)
        q_ref[r0 + HALF:r0 + HEAD_DIM, :] = (r2 * scale).astype(BF16)
    for kind in range(N_KINDS):
        for gi in range(N_KV_HEADS):
            r0 = gi * HEAD_DIM
            blk = pt_s[D_Q + kind * KV_DIM + r0:D_Q + kind * KV_DIM + r0 + HEAD_DIM, :]
            if kind % 2 == 0:
                r1, r2 = norm_rope(blk, kg_ref[kind // 2])
                blk = jnp.concatenate([r1, r2], axis=0)
            kv_ref[kind, r0:r0 + HEAD_DIM, :] = blk
            if kind >= 2:
                kvb_ref[kind - 2, r0:r0 + HEAD_DIM, :] = blk.astype(BF16)
            elif paged:
                dst = kcp_ref if kind == 0 else vcp_ref
                for p in range(tm // PAGE_SIZE):
                    dst[p, gi] = blk[:, p * PAGE_SIZE:(p + 1) * PAGE_SIZE]
    g0 = D_Q + N_KINDS * KV_DIM
    gt_ref[...] = jax.nn.sigmoid(pt_s[g0:g0 + N_GATES, :])


def _nsa_project(x, pos, g, w_in, q_gain, k_gain, *, tm, paged):
    S, R, D = x.shape
    NP = w_in.shape[1]
    assert R % tm == 0 and (not paged or tm % PAGE_SIZE == 0)
    nr = R // tm
    wt = w_in.T.astype(BF16)
    cos, sin = _rope_tables(pos)
    qg = jnp.broadcast_to(q_gain.astype(F32)[:, None], (HEAD_DIM, tm))
    kg = jnp.broadcast_to(k_gain.astype(F32)[:, :, None], (3, HEAD_DIM, tm))
    out_specs = [
        pl.BlockSpec((None, D_Q, tm), lambda s, i: (s, 0, i)),
        pl.BlockSpec((None, N_KINDS, KV_DIM, tm), lambda s, i: (s, 0, 0, i)),
        pl.BlockSpec((None, 4, KV_DIM, tm), lambda s, i: (s, 0, 0, i)),
        pl.BlockSpec((None, N_GATES, tm), lambda s, i: (s, 0, i)),
    ]
    out_shape = [
        jax.ShapeDtypeStruct((S, D_Q, R), BF16),
        jax.ShapeDtypeStruct((S, N_KINDS, KV_DIM, R), F32),
        jax.ShapeDtypeStruct((S, 4, KV_DIM, R), BF16),
        jax.ShapeDtypeStruct((S, N_GATES, R), F32),
    ]
    if paged:
        ppt = tm // PAGE_SIZE
        for _ in range(2):
            out_specs.append(pl.BlockSpec((ppt, N_KV_HEADS, HEAD_DIM, PAGE_SIZE),
                                          lambda s, i: (s * nr + i, 0, 0, 0)))
            out_shape.append(jax.ShapeDtypeStruct((S * R // PAGE_SIZE, N_KV_HEADS, HEAD_DIM, PAGE_SIZE), F32))
    return pl.pallas_call(
        functools.partial(_nsa_proj_kernel, tm=tm, paged=paged),
        grid=(S, nr),
        in_specs=[
            pl.BlockSpec((None, tm, D), lambda s, i: (s, i, 0)),
            pl.BlockSpec((1, D), lambda s, i: (0, 0)),
            pl.BlockSpec((NP, D), lambda s, i: (0, 0)),
            pl.BlockSpec((HEAD_DIM, tm), lambda s, i: (0, 0)),
            pl.BlockSpec((3, HEAD_DIM, tm), lambda s, i: (0, 0, 0)),
            pl.BlockSpec((HALF, tm), lambda s, i: (0, i)),
            pl.BlockSpec((HALF, tm), lambda s, i: (0, i)),
        ],
        out_specs=out_specs,
        out_shape=out_shape,
        scratch_shapes=[pltpu.VMEM((NP, tm), F32)],
        compiler_params=_cparams(("arbitrary", "arbitrary")),
        name="nsa_project",
    )(x, g.reshape(1, D), wt, qg, kg, cos, sin)


PAGE_ROWS = N_KV_HEADS * HEAD_DIM
CMP_PER_PAGE = PAGE_SIZE // L_CMP


def _page_copies(pages_hbm, buf, sem, page, slot, p, by_feature):
    if not by_feature:
        return [pltpu.make_async_copy(pages_hbm.at[page], buf.at[slot, pl.ds(p * PAGE_ROWS, PAGE_ROWS), :],
                                      sem.at[slot])]
    return [pltpu.make_async_copy(pages_hbm.at[page, pl.ds(g * HEAD_DIM, HEAD_DIM), :],
                                  buf.at[slot, :, p * N_KV_HEADS + g, :], sem.at[slot])
            for g in range(N_KV_HEADS)]


def _fetch_chunk(pt_ref, pages_hbm, buf, sem, n, slot, *, nchunks, npc, by_feature):
    s = n // nchunks
    c = n % nchunks

    def start(p, carry):
        for cp in _page_copies(pages_hbm, buf, sem, pt_ref[s, c * npc + p], slot, p, by_feature):
            cp.start()
        return carry

    lax.fori_loop(0, npc, start, 0)


def _wait_chunk(pages_hbm, buf, sem, slot, *, npc, by_feature):
    def wait(p, carry):
        for cp in _page_copies(pages_hbm, buf, sem, 0, slot, p, by_feature):
            cp.wait()
        return carry

    lax.fori_loop(0, npc, wait, 0)


def _stream_pages(pt_ref, pages_hbm, buf, sem, *, nchunks, npc, total, by_feature=False):
    n = pl.program_id(0) * nchunks + pl.program_id(1)
    slot = n % 2
    kw = dict(nchunks=nchunks, npc=npc, by_feature=by_feature)

    @pl.when(n == 0)
    def _():
        _fetch_chunk(pt_ref, pages_hbm, buf, sem, n, slot, **kw)

    @pl.when(n + 1 < total)
    def _():
        _fetch_chunk(pt_ref, pages_hbm, buf, sem, n + 1, 1 - slot, **kw)

    _wait_chunk(pages_hbm, buf, sem, slot, npc=npc, by_feature=by_feature)
    return slot


def _compress_tables(w1, b1, w2):
    eye = jnp.eye(CMP_PER_PAGE, dtype=F32)
    m = jnp.einsum('lde,ck->dclke', w1.astype(F32), eye)
    m = m.reshape(HEAD_DIM * PAGE_SIZE, CMP_PER_PAGE * HEAD_DIM).astype(BF16)
    b1t = jnp.tile(b1.astype(F32), CMP_PER_PAGE).reshape(1, CMP_PER_PAGE * HEAD_DIM)
    w2bd = jnp.einsum('ef,ck->cekf', w2.astype(F32), eye)
    w2bd = w2bd.reshape(CMP_PER_PAGE * HEAD_DIM, CMP_PER_PAGE * HEAD_DIM).astype(BF16)
    return m, b1t, w2bd


def _compress_kernel(pt_ref, pages_hbm, m_ref, b1_ref, w2_ref, o_ref, buf, sem, lhs_s, *, nchunks, npc, total):
    slot = _stream_pages(pt_ref, pages_hbm, buf, sem, nchunks=nchunks, npc=npc, total=total, by_feature=True)
    for d in range(HEAD_DIM):
        lhs_s[:, d * PAGE_SIZE:(d + 1) * PAGE_SIZE] = buf[slot, d].astype(BF16)
    acc = jnp.dot(lhs_s[...], m_ref[...], preferred_element_type=F32)
    h = jax.nn.gelu(acc + b1_ref[...])
    o_ref[...] = jnp.dot(h.astype(BF16), w2_ref[...], preferred_element_type=F32)


def _compress(pages, page_table, tables, *, npc):
    S, NPG = page_table.shape
    assert NPG % npc == 0
    nchunks = NPG // npc
    m, b1t, w2bd = tables
    nrow = npc * N_KV_HEADS
    nce = CMP_PER_PAGE * HEAD_DIM
    pages2 = pages.reshape(pages.shape[0], PAGE_ROWS, PAGE_SIZE)
    kern = functools.partial(_compress_kernel, nchunks=nchunks, npc=npc, total=S * nchunks)
    out = pl.pallas_call(
        kern,
        grid_spec=pltpu.PrefetchScalarGridSpec(
            num_scalar_prefetch=1,
            grid=(S, nchunks),
            in_specs=[
                pl.BlockSpec(memory_space=pl.ANY),
                pl.BlockSpec(m.shape, lambda s, c, pt: (0, 0)),
                pl.BlockSpec(b1t.shape, lambda s, c, pt: (0, 0)),
                pl.BlockSpec(w2bd.shape, lambda s, c, pt: (0, 0)),
            ],
            out_specs=pl.BlockSpec((None, nrow, nce), lambda s, c, pt: (s, c, 0)),
            scratch_shapes=[pltpu.VMEM((2, HEAD_DIM, nrow, PAGE_SIZE), F32), pltpu.SemaphoreType.DMA((2,)),
                            pltpu.VMEM((nrow, HEAD_DIM * PAGE_SIZE), BF16)],
        ),
        out_shape=jax.ShapeDtypeStruct((S, NPG * N_KV_HEADS, nce), F32),
        compiler_params=_cparams(("arbitrary", "arbitrary")),
        name="nsa_compress",
    )(page_table, pages2, m, b1t, w2bd)
    out = out.reshape(S, NPG, N_KV_HEADS, CMP_PER_PAGE, HEAD_DIM).transpose(0, 1, 3, 2, 4)
    return out.reshape(S, NPG * CMP_PER_PAGE, N_KV_HEADS, HEAD_DIM)


HPG = N_HEADS // N_KV_HEADS
CMP_PER_SEL = L_SEL // L_CMP
NT_DIMS = (((1,), (1,)), ((), ()))


def _masked_softmax(s, mask):
    s = jnp.where(mask, s, NEG_INF)
    e = jnp.exp(s - jnp.max(s, axis=-1, keepdims=True))
    return jnp.where(mask, e / jnp.sum(e, axis=-1, keepdims=True), 0.0)


def _select_blocks(imp, pos_q, n_sel):
    nq = imp.shape[0]
    jidx = lax.broadcasted_iota(jnp.int32, (nq, n_sel), 1)
    qblk = pos_q // L_SEL
    valid = jidx * L_SEL <= pos_q
    forced = (jidx == 0) | (jidx == qblk) | (jidx == qblk - 1)
    score = jnp.where(valid & forced, FORCE_SCORE, jnp.where(valid, imp, NEG_INF))
    sct = score.T
    jrow = lax.broadcasted_iota(jnp.int32, (n_sel, nq), 0)
    sel = jnp.zeros((n_sel, nq), F32)
    for _ in range(min(TOP_N, n_sel)):
        top = jnp.max(sct, axis=0, keepdims=True)
        first = jnp.min(jnp.where(sct == top, jrow, n_sel), axis=0, keepdims=True)
        hit = jrow == first
        sct = jnp.where(hit, -jnp.inf, sct)
        sel = jnp.where(hit, 1.0, sel)
    return sel


def _cmp_block_of_column(ncol):
    c = lax.broadcasted_iota(jnp.int32, (1, ncol), 1)
    half = ncol // CMP_PER_SEL
    return jnp.where(c < half, CMP_PER_SEL * c, CMP_PER_SEL * (c - half) + 1)


Q_BLOCK = 128
GROUPS_PER_STEP = 4
BF16_SUBLANES = 2 * SUBLANES


M_FLOOR = 0.1 * NEG_INF


def _bias_softmax(s, bias):
    s = s + bias
    e = jnp.exp(s - jnp.maximum(jnp.max(s, axis=-1, keepdims=True), M_FLOOR))
    l = jnp.sum(e, axis=-1, keepdims=True)
    return e, 1.0 / jnp.where(l > 0.0, l, 1.0)


def _attn_prompt_kernel(q_ref, kc_ref, vc_ref, ksa_ref, vsa_ref, kw_ref, vwa_ref, gt_ref, o_ref, *, tk, T, gb):
    qi = pl.program_id(2)
    nq = Q_BLOCK
    nr = HPG * nq
    nc = kc_ref.shape[2]
    n_sel = nc // CMP_PER_SEL
    va = vsa_ref.shape[1]
    pos_q = qi * nq + lax.broadcasted_iota(jnp.int32, (nq, 1), 0)
    cbias = jnp.where((_cmp_block_of_column(nc) * L_CMP + (L_CMP - 1)) <= pos_q, 0.0, NEG_INF)

    def normalise(r):
        l = r[..., HEAD_DIM:HEAD_DIM + 1]
        return r[..., :HEAD_DIM] * (1.0 / jnp.where(l > 0.0, l, 1.0))

    qs, q_augs, o_cs = [], [], []
    for gl in range(gb):
        qrows = q_ref[gl].astype(F32).T
        q = jnp.concatenate([qrows[:, h * HEAD_DIM:(h + 1) * HEAD_DIM] for h in range(HPG)], axis=0)
        q = q.astype(BF16)
        s_c = jnp.dot(q, kc_ref[gl], preferred_element_type=F32).reshape(HPG, nq, nc)
        e_c, inv_c = _bias_softmax(s_c, cbias[None])
        p_c = e_c * inv_c
        o_cs.append(jnp.dot(p_c.reshape(nr, nc).astype(BF16), vc_ref[gl],
                            preferred_element_type=F32).reshape(HPG, nq, HEAD_DIM))
        imp = p_c[0]
        for h in range(1, HPG):
            imp = imp + p_c[h]
        imp = imp[:, :n_sel] + imp[:, n_sel:]
        sel = _select_blocks(imp, pos_q, n_sel).T
        unsel = jnp.where(sel > 0.5, 0.0, NEG_INF).astype(BF16)
        qs.append(q)
        q_augs.append(jnp.concatenate([jnp.concatenate([unsel] * HPG, axis=0), q], axis=1))

    def sel_tile(gl, k0, carry, causal):
        m, acc = carry
        s = jnp.dot(q_augs[gl], ksa_ref[gl, :, pl.ds(k0, tk)], preferred_element_type=F32)
        s = s.reshape(HPG, nq, tk)
        if causal:
            kpos = k0 + lax.broadcasted_iota(jnp.int32, (nq, tk), 1)
            s = s + jnp.where(kpos <= pos_q, 0.0, NEG_INF)[None]
        m_new = jnp.maximum(m, jnp.max(s, axis=-1, keepdims=True))
        p = jnp.exp(s - m_new).astype(BF16)
        pv = lax.dot_general(p.reshape(nr, tk), vsa_ref[gl, :, pl.ds(k0, tk)], NT_DIMS,
                             preferred_element_type=F32)
        return m_new, jnp.exp(m - m_new) * acc + pv.reshape(HPG, nq, va)

    def sel_tiles(k0, carries, causal):
        return tuple(sel_tile(gl, k0, carries[gl], causal) for gl in range(gb))

    last = (qi * nq + nq + tk - 1) // tk - 1
    init = (jnp.full((HPG, nq, 1), M_FLOOR, F32), jnp.zeros((HPG, nq, va), F32))
    carries = lax.fori_loop(0, last, lambda kt, c: sel_tiles(pl.multiple_of(kt * tk, tk), c, False),
                            (init,) * gb)
    carries = sel_tiles(pl.multiple_of(last * tk, tk), carries, True)

    wl = min(WINDOW + nq, T)
    w0 = pl.multiple_of(jnp.maximum(qi * nq + nq - wl, 0), LANES)
    dpos = pos_q - (w0 + lax.broadcasted_iota(jnp.int32, (nq, wl), 1))
    wbias = jnp.where((dpos >= 0) & (dpos < WINDOW), 0.0, NEG_INF)[None]
    for gl in range(gb):
        o_s = normalise(carries[gl][1])
        s_w = jnp.dot(qs[gl], kw_ref[gl, :, pl.ds(w0, wl)], preferred_element_type=F32).reshape(HPG, nq, wl)
        s_w = s_w + wbias
        e_w = jnp.exp(s_w - jnp.maximum(jnp.max(s_w, axis=-1, keepdims=True), M_FLOOR)).astype(BF16)
        o_w = normalise(lax.dot_general(e_w.reshape(nr, wl), vwa_ref[gl, :, pl.ds(w0, wl)], NT_DIMS,
                                        preferred_element_type=F32).reshape(HPG, nq, va))
        gt = gt_ref[gl]
        o_c = o_cs[gl]
        outs = []
        for h in range(HPG):
            outs.append(gt[:, h:h + 1] * o_c[h] + gt[:, HPG + h:HPG + h + 1] * o_s[h]
                        + gt[:, 2 * HPG + h:2 * HPG + h + 1] * o_w[h])
        o_ref[:, gl * HPG * HEAD_DIM:(gl + 1) * HPG * HEAD_DIM] = jnp.concatenate(outs, axis=1).astype(BF16)


def _block_expansion(n_sel, n_keys):
    return (jnp.arange(n_keys)[None, :] // L_SEL == jnp.arange(n_sel)[:, None]).astype(BF16)


def _cmp_layouts(kc, vc):
    order = jnp.concatenate([jnp.arange(0, kc.shape[1], CMP_PER_SEL), jnp.arange(1, kc.shape[1], CMP_PER_SEL)])
    kct = kc[:, order].transpose(0, 2, 3, 1).astype(BF16)
    vcr = vc[:, order].transpose(0, 2, 1, 3).astype(BF16)
    return kct, vcr


def _attend_prompt(qT, kvbT, gT, kc, vc, *, tk):
    B, _, T = qT.shape
    G = N_KV_HEADS
    nq = Q_BLOCK
    assert T % tk == 0 and T % nq == 0 and tk % L_SEL == 0
    nc = T // L_CMP
    n_sel = T // L_SEL
    kct, vcr = _cmp_layouts(kc, vc)
    q4 = qT.reshape(B, G, HPG * HEAD_DIM, T)
    kv5 = kvbT.reshape(B, 4, G, HEAD_DIM, T)
    gates = gT.reshape(B, 3, G, HPG, T).transpose(0, 2, 4, 1, 3).reshape(B, G, T, 3 * HPG)
    e = jnp.broadcast_to(_block_expansion(n_sel, T)[None, None], (B, G, n_sel, T))
    ks_aug = jnp.concatenate([e, kv5[:, 0]], axis=2)
    ones_pad = jnp.zeros((B, G, BF16_SUBLANES, T), BF16).at[:, :, 0].set(1.0)
    vs_aug = jnp.concatenate([kv5[:, 1], ones_pad], axis=2)
    vw_aug = jnp.concatenate([kv5[:, 3], ones_pad], axis=2)
    va = HEAD_DIM + BF16_SUBLANES
    gb = GROUPS_PER_STEP
    assert G % gb == 0
    once = pl.Buffered(1)
    per_group = lambda rows: pl.BlockSpec((None, gb, rows, T), lambda b, g, i: (b, g, 0, 0), pipeline_mode=once)
    return pl.pallas_call(
        functools.partial(_attn_prompt_kernel, tk=tk, T=T, gb=gb),
        grid=(B, G // gb, T // nq),
        in_specs=[
            pl.BlockSpec((None, gb, HPG * HEAD_DIM, nq), lambda b, g, i: (b, g, 0, i)),
            pl.BlockSpec((None, gb, HEAD_DIM, nc), lambda b, g, i: (b, g, 0, 0)),
            pl.BlockSpec((None, gb, nc, HEAD_DIM), lambda b, g, i: (b, g, 0, 0)),
            per_group(n_sel + HEAD_DIM), per_group(va),
            pl.BlockSpec((None, None, gb, HEAD_DIM, T), lambda b, g, i: (b, 2, g, 0, 0), pipeline_mode=once),
            per_group(va),
            pl.BlockSpec((None, gb, nq, 3 * HPG), lambda b, g, i: (b, g, i, 0)),
        ],
        out_specs=pl.BlockSpec((None, nq, gb * HPG * HEAD_DIM), lambda b, g, i: (b, i, g)),
        out_shape=jax.ShapeDtypeStruct((B, T, D_Q), BF16),
        compiler_params=_cparams(("arbitrary", "arbitrary", "arbitrary")),
        name="nsa_attend_prompt",
    )(q4, kct, vcr, ks_aug, vs_aug, kv5, vw_aug, gates)


PAGE_LOOP_UNROLL = 4


def _attn_sample_kernel(pt_ref, q_ref, kc_ref, vc_ref, kwc_ref, vwc_ref, kwn_ref, vwn_ref, ksn_ref, vsn_ref,
                        gt_ref, e_ref, dm_ref, ks_hbm, vs_hbm, o_ref,
                        kbuf, vbuf, ksem, vsem, s_s, sel_s, m_s, l_s, acc_s, oc_s, ow_s,
                        *, nchunks, npc, total, past, tq):
    c = pl.program_id(1)
    kslot = _stream_pages(pt_ref, ks_hbm, kbuf, ksem, nchunks=nchunks, npc=npc, total=total)
    vslot = _stream_pages(pt_ref, vs_hbm, vbuf, vsem, nchunks=nchunks, npc=npc, total=total)
    q = q_ref[...]
    nr = q.shape[0]
    ngt = N_KV_HEADS * tq
    pos_q = past + lax.broadcasted_iota(jnp.int32, (nr, 1), 0) % tq
    n_sel = sel_s.shape[1]
    sel_past = past // L_SEL

    def fold(o_full):
        o = o_full * dm_ref[...]
        out = o[:, 0:HEAD_DIM]
        for gi in range(1, N_KV_HEADS):
            out = out + o[:, gi * HEAD_DIM:(gi + 1) * HEAD_DIM]
        return out

    def nt(p, vt):
        return lax.dot_general(p.astype(BF16), vt.astype(BF16), NT_DIMS, preferred_element_type=F32)

    @pl.when(c == 0)
    def _():
        nc = kc_ref.shape[1]
        s_c = jnp.dot(q, kc_ref[...], preferred_element_type=F32)
        cmask = (_cmp_block_of_column(nc) * L_CMP + (L_CMP - 1)) <= pos_q
        p_c = _masked_softmax(s_c, cmask)
        oc_s[...] = fold(nt(p_c, vc_ref[...]))
        imp = p_c[0:ngt]
        for h in range(1, HPG):
            imp = imp + p_c[h * ngt:(h + 1) * ngt]
        imp = imp[:, :n_sel] + imp[:, n_sel:]
        pad = LANES - ngt
        imp = jnp.concatenate([imp, jnp.zeros((pad, n_sel), F32)], axis=0)
        pos_pad = past + lax.broadcasted_iota(jnp.int32, (LANES, 1), 0) % tq
        sel = _select_blocks(imp, pos_pad, n_sel).T[0:ngt]
        sel_s[...] = jnp.concatenate([sel] * HPG, axis=0)

        nwc = kwc_ref.shape[1]
        nwn = kwn_ref.shape[1]
        s_w = jnp.concatenate([jnp.dot(q, kwc_ref[...].astype(BF16), preferred_element_type=F32),
                               jnp.dot(q, kwn_ref[...].astype(BF16), preferred_element_type=F32)], axis=1)
        lane = lax.broadcasted_iota(jnp.int32, (nr, nwc + nwn), 1)
        pos_w = jnp.where(lane < nwc, past - nwc + lane, past + lane - nwc)
        dpos = pos_q - pos_w
        p_w = _masked_softmax(s_w, (dpos >= 0) & (dpos < WINDOW) & (pos_w >= 0))
        ow_s[...] = fold(nt(p_w[:, :nwc], vwc_ref[...]) + nt(p_w[:, nwc:], vwn_ref[...]))

        nsn = ksn_ref.shape[1]
        s_n = jnp.dot(q, ksn_ref[...].astype(BF16), preferred_element_type=F32)
        kpos = past + lax.broadcasted_iota(jnp.int32, (nr, nsn), 1)
        msk = (sel_s[:, sel_past:sel_past + 1] > 0.5) & (kpos <= pos_q)
        s_n = jnp.where(msk, s_n, NEG_INF)
        m0 = jnp.max(s_n, axis=-1, keepdims=True)
        p_n = jnp.where(msk, jnp.exp(s_n - m0), 0.0)
        m_s[...] = m0
        l_s[...] = jnp.sum(p_n, axis=-1, keepdims=True)
        acc_s[...] = nt(p_n, vsn_ref[...])

    def qk(p, carry):
        r0 = pl.multiple_of(p * PAGE_ROWS, PAGE_ROWS)
        c0 = pl.multiple_of(p * PAGE_SIZE, PAGE_SIZE)
        s_s[:, pl.ds(c0, PAGE_SIZE)] = jnp.dot(q, kbuf[kslot, pl.ds(r0, PAGE_ROWS), :].astype(BF16),
                                                preferred_element_type=F32)
        return carry

    lax.fori_loop(0, npc, qk, 0, unroll=PAGE_LOOP_UNROLL)
    nk = npc * PAGE_SIZE
    sel_c = sel_s[:, pl.ds(pl.multiple_of(c * (nk // L_SEL), LANES), nk // L_SEL)]
    chosen = jnp.dot(sel_c.astype(BF16), e_ref[...], preferred_element_type=F32)
    kpos = c * nk + lax.broadcasted_iota(jnp.int32, (nr, nk), 1)
    msk = (chosen > 0.5) & (kpos <= pos_q)
    s = jnp.where(msk, s_s[...], NEG_INF)
    m_old = m_s[...]
    m_new = jnp.maximum(m_old, jnp.max(s, axis=-1, keepdims=True))
    alpha = jnp.exp(m_old - m_new)
    p = jnp.where(msk, jnp.exp(s - m_new), 0.0)
    l_s[...] = alpha * l_s[...] + jnp.sum(p, axis=-1, keepdims=True)
    m_s[...] = m_new
    s_s[...] = p

    def pv(p_, acc):
        r0 = pl.multiple_of(p_ * PAGE_ROWS, PAGE_ROWS)
        c0 = pl.multiple_of(p_ * PAGE_SIZE, PAGE_SIZE)
        return acc + nt(s_s[:, pl.ds(c0, PAGE_SIZE)], vbuf[vslot, pl.ds(r0, PAGE_ROWS), :])

    acc = lax.fori_loop(0, npc, pv, jnp.zeros((nr, PAGE_ROWS), F32), unroll=PAGE_LOOP_UNROLL)
    acc_s[...] = alpha * acc_s[...] + acc

    @pl.when(c == nchunks - 1)
    def _():
        l = l_s[...]
        o_sel = fold(acc_s[...]) / jnp.where(l > 0.0, l, 1.0)
        gt = gt_ref[...]
        o_ref[...] = gt[:, 0:1] * oc_s[...] + gt[:, 1:2] * o_sel + gt[:, 2:3] * ow_s[...]


def _attend_sample(qT, kvT, gT, kc_all, vc_all, win_k, win_v, pool_ks, pool_vs, page_table, *, npc):
    Bs, NPG = page_table.shape
    tq = qT.shape[1] // Bs
    G = N_KV_HEADS
    past = NPG * PAGE_SIZE
    nchunks = NPG // npc
    nk = npc * PAGE_SIZE
    assert NPG % npc == 0 and nk // L_SEL == LANES
    nr = HPG * G * tq
    eye = jnp.eye(G, dtype=qT.dtype)
    q5 = qT.reshape(G, HPG, HEAD_DIM, tq, Bs).transpose(4, 1, 0, 3, 2)
    q_bd = jnp.einsum('bhgtd,gk->bhgtkd', q5, eye).reshape(Bs, nr, G * HEAD_DIM)
    dm = jnp.broadcast_to(jnp.eye(G, dtype=F32)[None, :, None, :, None],
                          (HPG, G, tq, G, HEAD_DIM)).reshape(nr, G * HEAD_DIM)
    gates = gT.reshape(3, G, HPG, tq, Bs).transpose(4, 2, 1, 3, 0).reshape(Bs, nr, 3)

    def new_rows(kind):
        r = kvT[kind].reshape(G * HEAD_DIM, tq, Bs).transpose(2, 0, 1)
        return jnp.pad(r, ((0, 0), (0, 0), (0, LANES - tq)))

    nc = kc_all.shape[1]
    half = nc // CMP_PER_SEL
    n_sel = -(-(half) // LANES) * LANES
    assert past // L_SEL < n_sel

    def cmp_fm(x):
        padw = ((0, 0), (0, 0), (0, n_sel - half))
        parts = [jnp.pad(x[:, par::CMP_PER_SEL].transpose(0, 2, 3, 1).reshape(Bs, G * HEAD_DIM, half), padw)
                 for par in range(CMP_PER_SEL)]
        return jnp.concatenate(parts, axis=2).astype(BF16)

    n_win = win_k.shape[-1]
    wk = win_k.reshape(Bs, G * HEAD_DIM, n_win)
    wv = win_v.reshape(Bs, G * HEAD_DIM, n_win)
    e = _block_expansion(nk // L_SEL, nk)
    ks2 = pool_ks.reshape(pool_ks.shape[0], PAGE_ROWS, PAGE_SIZE)
    vs2 = pool_vs.reshape(pool_vs.shape[0], PAGE_ROWS, PAGE_SIZE)
    per_b = lambda shape: pl.BlockSpec((None,) + shape, lambda b, c, pt: (b, 0, 0))
    kern = functools.partial(_attn_sample_kernel, nchunks=nchunks, npc=npc, total=Bs * nchunks, past=past, tq=tq)
    o = pl.pallas_call(
        kern,
        grid_spec=pltpu.PrefetchScalarGridSpec(
            num_scalar_prefetch=1,
            grid=(Bs, nchunks),
            in_specs=[
                per_b((nr, G * HEAD_DIM)),
                per_b((G * HEAD_DIM, 2 * n_sel)), per_b((G * HEAD_DIM, 2 * n_sel)),
                per_b((G * HEAD_DIM, n_win)), per_b((G * HEAD_DIM, n_win)),
                per_b((G * HEAD_DIM, LANES)), per_b((G * HEAD_DIM, LANES)),
                per_b((G * HEAD_DIM, LANES)), per_b((G * HEAD_DIM, LANES)),
                per_b((nr, 3)),
                pl.BlockSpec(e.shape, lambda b, c, pt: (0, 0)),
                pl.BlockSpec(dm.shape, lambda b, c, pt: (0, 0)),
                pl.BlockSpec(memory_space=pl.ANY),
                pl.BlockSpec(memory_space=pl.ANY),
            ],
            out_specs=pl.BlockSpec((None, nr, HEAD_DIM), lambda b, c, pt: (b, 0, 0)),
            scratch_shapes=[
                pltpu.VMEM((2, npc * PAGE_ROWS, PAGE_SIZE), F32), pltpu.VMEM((2, npc * PAGE_ROWS, PAGE_SIZE), F32),
                pltpu.SemaphoreType.DMA((2,)), pltpu.SemaphoreType.DMA((2,)),
                pltpu.VMEM((nr, nk), F32), pltpu.VMEM((nr, n_sel), F32),
                pltpu.VMEM((nr, 1), F32), pltpu.VMEM((nr, 1), F32), pltpu.VMEM((nr, G * HEAD_DIM), F32),
                pltpu.VMEM((nr, HEAD_DIM), F32), pltpu.VMEM((nr, HEAD_DIM), F32),
            ],
        ),
        out_shape=jax.ShapeDtypeStruct((Bs, nr, HEAD_DIM), F32),
        compiler_params=_cparams(("arbitrary", "arbitrary")),
        name="nsa_attend_sample",
    )(page_table, q_bd, cmp_fm(kc_all), cmp_fm(vc_all), wk, wv, new_rows(4), new_rows(5), new_rows(2), new_rows(3),
      gates, e, dm, ks2, vs2)
    o = o.reshape(Bs, HPG, G, tq, HEAD_DIM).transpose(3, 0, 2, 1, 4)
    return o.reshape(tq * Bs, D_Q).astype(BF16)


def _out_proj_kernel(a_ref, w_ref, x_ref, o_ref):
    o_ref[...] = x_ref[...] + jnp.dot(a_ref[...], w_ref[...], preferred_element_type=F32)


def _out_proj(a, w, x, *, tm):
    R, Kd = a.shape
    N = w.shape[1]
    assert R % tm == 0
    return pl.pallas_call(
        _out_proj_kernel,
        grid=(R // tm,),
        in_specs=[pl.BlockSpec((tm, Kd), lambda i: (i, 0)), pl.BlockSpec((Kd, N), lambda i: (0, 0)),
                  pl.BlockSpec((tm, N), lambda i: (i, 0))],
        out_specs=pl.BlockSpec((tm, N), lambda i: (i, 0)),
        out_shape=jax.ShapeDtypeStruct((R, N), F32),
        compiler_params=_cparams(("arbitrary",)),
        name="out_proj",
    )(a, w.astype(BF16), x)


PROMPT_ROW_TILE = 512
S5_CHUNK = 256
SEL_KEY_TILE = 1024
SAMPLE_PAGES_PER_CHUNK = 64


def _feature_major(cache):
    nd = cache.ndim
    return jnp.moveaxis(cache, nd - 3, nd - 1)


def _nsa_prompt_layer(x, g, w_in, q_gain, k_gain, ck, cv, w_out):
    B, T, D = x.shape
    G = N_KV_HEADS
    qT, kvT, kvbT, gT, kcp, vcp = _nsa_project(x, jnp.arange(T), g, w_in, q_gain, k_gain,
                                               tm=PROMPT_ROW_TILE, paged=True)
    npg = T // PAGE_SIZE
    ptab = jnp.arange(B * npg, dtype=jnp.int32).reshape(B, npg)
    kc = _compress(kcp, ptab, _compress_tables(*ck), npc=npg)
    vc = _compress(vcp, ptab, _compress_tables(*cv), npc=npg)
    o = _attend_prompt(qT, kvbT, gT, kc, vc, tk=SEL_KEY_TILE)
    y = _out_proj(o.reshape(B * T, D_Q), w_out, x.reshape(B * T, D), tm=PROMPT_ROW_TILE).reshape(B, T, D)
    rows = [kvT[:, k].reshape(B, G, HEAD_DIM, T).transpose(0, 3, 1, 2) for k in range(N_KINDS)]
    n_keep = min(WINDOW, T)
    rows[4] = rows[4][:, T - n_keep:]
    rows[5] = rows[5][:, T - n_keep:]
    return y, rows


def _nsa_sample_layer(x_tm, Bs, pools, win_k, win_v, page_table, g, w_in, q_gain, k_gain, ck, cv, w_out):
    _, R, D = x_tm.shape
    G = N_KV_HEADS
    tq = R // Bs
    npg = page_table.shape[1]
    past = npg * PAGE_SIZE
    pos = past + jnp.arange(R) // Bs
    qT, kvT, _, gT = _nsa_project(x_tm, pos, g, w_in, q_gain, k_gain, tm=R, paged=False)
    qT, kvT, gT = qT[0], kvT[0], gT[0]
    new_pad = -(-tq // L_SEL) * L_SEL
    assert new_pad <= PAGE_SIZE
    new_tab = jnp.arange(Bs, dtype=jnp.int32).reshape(1, Bs)

    def cmp_all(pool, kind, tabs):
        past_blocks = _compress(_feature_major(pool), page_table, tabs, npc=SAMPLE_PAGES_PER_CHUNK)
        page = kvT[kind].reshape(G, HEAD_DIM, tq, Bs).transpose(3, 0, 1, 2)
        page = jnp.pad(page, ((0, 0), (0, 0), (0, 0), (0, PAGE_SIZE - tq)))
        new_blocks = _compress(page, new_tab, tabs, npc=Bs).reshape(Bs, CMP_PER_PAGE, G, HEAD_DIM)
        return jnp.concatenate([past_blocks, new_blocks[:, :new_pad // L_CMP]], axis=1)

    kc_all = cmp_all(pools[0], 0, _compress_tables(*ck))
    vc_all = cmp_all(pools[1], 1, _compress_tables(*cv))
    o = _attend_sample(qT, kvT, gT, kc_all, vc_all, _feature_major(win_k), _feature_major(win_v),
                       _feature_major(pools[2]), _feature_major(pools[3]), page_table,
                       npc=SAMPLE_PAGES_PER_CHUNK)
    y = _out_proj(o, w_out, x_tm[0], tm=R)[None]
    rows = [kvT[k].reshape(G, HEAD_DIM, tq, Bs).transpose(3, 2, 0, 1) for k in range(N_KINDS)]
    return y, rows


def kernel(x_prompt, x_sample, cache_k_cmp, cache_v_cmp, cache_k_slc, cache_v_slc, cache_k_win, cache_v_win,
           state_ssm_re, state_ssm_im, state_conv, page_table, norm_mix, norm_ffn,
           ssm_w_in, ssm_a_re, ssm_a_im, ssm_log_dt, ssm_b_re, ssm_b_im, ssm_c_re, ssm_c_im,
           ssm_d, ssm_w_gate, ssm_w_out,
           nsa_w_in, nsa_q_gain, nsa_k_gain, nsa_ck_w1, nsa_ck_b1, nsa_ck_w2,
           nsa_cv_w1, nsa_cv_b1, nsa_cv_w2, nsa_w_out,
           ffn_w_up, ffn_conv_w, ffn_conv_b, ffn_w_down):
    B, T, D = x_prompt.shape
    Bs, Tq, _ = x_sample.shape
    depth = norm_mix.shape[0]
    Fd = ffn_w_down.shape[1]
    G, P = ssm_a_re.shape[1:]
    xp = x_prompt
    xs = x_sample.transpose(1, 0, 2).reshape(1, Tq * Bs, D)
    p_sre, p_sim, s_sre, s_sim, p_cv, s_cv = [], [], [], [], [], []
    p_rows = [[] for _ in range(N_KINDS)]
    s_rows = [[] for _ in range(N_KINDS)]
    for i in range(depth):
        li = i // 2
        if i % 2 == 0:
            tabs = _s5_tables(ssm_a_re[li], ssm_a_im[li], ssm_log_dt[li], ssm_b_re[li], ssm_b_im[li],
                              ssm_c_re[li], ssm_c_im[li])
            sp = (norm_mix[i], ssm_w_in[li], tabs, ssm_d[li], ssm_w_gate[li], ssm_w_out[li])
            z = jnp.zeros((B, SUBLANES, G * P), F32)
            xp, fr, fi = _s5_layer(xp, z, z, *sp, tc=min(S5_CHUNK, T), slab=1)
            p_sre.append(fr[:, SUBLANES - 1].reshape(B, G, P))
            p_sim.append(fi[:, SUBLANES - 1].reshape(B, G, P))
            xs, fr, fi = _s5_layer(xs, state_ssm_re[li].reshape(1, Bs, G * P).astype(F32),
                                   state_ssm_im[li].reshape(1, Bs, G * P).astype(F32), *sp, tc=Tq * Bs, slab=Bs)
            s_sre.append(fr.reshape(Bs, G, P))
            s_sim.append(fi.reshape(Bs, G, P))
        else:
            ap = (norm_mix[i], nsa_w_in[li], nsa_q_gain[li], nsa_k_gain[li],
                  (nsa_ck_w1[li], nsa_ck_b1[li], nsa_ck_w2[li]), (nsa_cv_w1[li], nsa_cv_b1[li], nsa_cv_w2[li]),
                  nsa_w_out[li])
            xp, rp = _nsa_prompt_layer(xp, *ap)
            xs, rs = _nsa_sample_layer(xs, Bs, (cache_k_cmp[li], cache_v_cmp[li], cache_k_slc[li], cache_v_slc[li]),
                                       cache_k_win[li], cache_v_win[li], page_table, *ap)
            for j in range(N_KINDS):
                p_rows[j].append(rp[j])
                s_rows[j].append(rs[j])
        fw = (norm_ffn[i], ffn_w_up[i], ffn_conv_w[i], ffn_conv_b[i], ffn_w_down[i])
        xp, cp = _conv_ffn(xp, jnp.zeros((B, SUBLANES, Fd), F32), *fw, shift=1, tm=min(PROMPT_ROW_TILE, T))
        p_cv.append(cp[:, SUBLANES - (CONV_W - 1):])
        init = state_conv[i].astype(F32).transpose(1, 0, 2).reshape(1, (CONV_W - 1) * Bs, Fd)
        xs, cs = _conv_ffn(xs, init, *fw, shift=Bs, tm=Tq * Bs)
        s_cv.append(cs.reshape(CONV_W - 1, Bs, Fd).transpose(1, 0, 2))
    st = jnp.stack
    y_sample = xs.reshape(Tq, Bs, D).transpose(1, 0, 2)
    return (xp, y_sample,
            st(p_sre), st(p_sim), *[st(r) for r in p_rows], st(p_cv),
            st(s_sre), st(s_sim), *[st(r) for r in s_rows], st(s_cv))
```

```python
import functools
import math

import jax
import jax.numpy as jnp
from jax import lax
from jax.experimental import pallas as pl
from jax.experimental.pallas import tpu as pltpu

F32 = jnp.float32
BF16 = jnp.bfloat16

NORM_EPS = 1e-6
SSM_CH = 16
SSM_STATE = 64
N_HEADS = 16
N_KV_HEADS = 4
HEAD_DIM = 64
L_CMP = 32
L_SEL = 64
TOP_N = 16
WINDOW = 512
PAGE_SIZE = 128
ROPE_THETA = 10000.0
NEG_INF = -1e30
FORCE_SCORE = 1e4
CONV_W = 3

LANES = 128
SUBLANES = 8
VMEM_LIMIT = 56 * 1024 * 1024


def _cparams(sem):
    return pltpu.CompilerParams(dimension_semantics=sem, vmem_limit_bytes=VMEM_LIMIT)


def _rms(x, g):
    return x * lax.rsqrt(jnp.mean(x * x, axis=-1, keepdims=True) + NORM_EPS) * g


FFN_MIN_SUBTILE = 256
FFN_CHUNKS = 1


def _ffn_kernel(x_ref, g_ref, wa_ref, wb_ref, cw_ref, cb_ref, wd_ref, init_ref,
                o_ref, buf_ref, xn_s, acc_s, aext_s, carry_s, *, shift, base, tm, nf, nsub):
    i = pl.program_id(1)
    j = pl.program_id(2)

    @pl.when(j == 0)
    def _():
        x = x_ref[...]
        xn_s[...] = _rms(x, g_ref[...]).astype(BF16)
        acc_s[...] = x

    @pl.when(i == 0)
    def _():
        aext_s[0:base, :] = init_ref[...]

    @pl.when(i > 0)
    def _():
        aext_s[0:base, :] = carry_s[j]

    cw = cw_ref[...]
    ts = tm // nsub
    for sub in range(nsub):
        r0 = sub * ts
        xn = xn_s[r0:r0 + ts, :]
        a = jnp.dot(xn, wa_ref[...], preferred_element_type=F32)
        b = jnp.dot(xn, wb_ref[...], preferred_element_type=F32)
        aext_s[base + r0:base + r0 + ts, :] = a
        a1 = aext_s[base - shift + r0:base - shift + r0 + ts, :]
        a2 = aext_s[base - 2 * shift + r0:base - 2 * shift + r0 + ts, :]
        c = cb_ref[...] + cw[0:1, :] * a2
        c = c + cw[1:2, :] * a1
        c = c + cw[2:3, :] * a
        h = (jax.nn.silu(c) * b).astype(BF16)
        acc_s[r0:r0 + ts, :] += jnp.dot(h, wd_ref[...], preferred_element_type=F32)
    tail = aext_s[tm:tm + base, :]
    carry_s[j] = tail
    buf_ref[...] = tail

    @pl.when(j == nf - 1)
    def _():
        o_ref[...] = acc_s[...]


def _conv_ffn(x, init, g, w_up, conv_w, conv_b, w_down, *, shift, tm):
    S, R, D = x.shape
    Fd = w_down.shape[0]
    base = init.shape[1]
    nf = FFN_CHUNKS
    tf = Fd // nf
    assert tf * nf == Fd and tf % LANES == 0 and R % tm == 0 and base >= 2 * shift
    wu = w_up.astype(BF16)
    wd = w_down.astype(BF16)
    nsub = 2 if tm % (2 * FFN_MIN_SUBTILE) == 0 else 1
    kern = functools.partial(_ffn_kernel, shift=shift, base=base, tm=tm, nf=nf, nsub=nsub)
    wmode = pl.Buffered(1) if nf == 1 else None
    out, buf = pl.pallas_call(
        kern,
        grid=(S, R // tm, nf),
        in_specs=[
            pl.BlockSpec((None, tm, D), lambda s, i, j: (s, i, 0)),
            pl.BlockSpec((1, D), lambda s, i, j: (0, 0)),
            pl.BlockSpec((D, tf), lambda s, i, j: (0, j), pipeline_mode=wmode),
            pl.BlockSpec((D, tf), lambda s, i, j: (0, nf + j), pipeline_mode=wmode),
            pl.BlockSpec((CONV_W, tf), lambda s, i, j: (0, j)),
            pl.BlockSpec((1, tf), lambda s, i, j: (0, j)),
            pl.BlockSpec((tf, D), lambda s, i, j: (j, 0), pipeline_mode=wmode),
            pl.BlockSpec((None, base, tf), lambda s, i, j: (s, 0, j)),
        ],
        out_specs=[
            pl.BlockSpec((None, tm, D), lambda s, i, j: (s, i, 0)),
            pl.BlockSpec((None, None, base, tf), lambda s, i, j: (s, i, 0, j)),
        ],
        out_shape=[jax.ShapeDtypeStruct((S, R, D), F32), jax.ShapeDtypeStruct((S, R // tm, base, Fd), F32)],
        scratch_shapes=[
            pltpu.VMEM((tm, D), BF16),
            pltpu.VMEM((tm, D), F32),
            pltpu.VMEM((tm + base, tf), F32),
            pltpu.VMEM((nf, base, tf), F32),
        ],
        compiler_params=_cparams(("arbitrary", "arbitrary", "arbitrary")),
        name="conv_ffn",
    )(x, g.reshape(1, D), wu, wu, conv_w, conv_b.reshape(1, Fd), wd, init)
    return out, buf[:, R // tm - 1]


def _s5_tables(a_re, a_im, log_dt, b_re, b_im, c_re, c_im):
    G, P = a_re.shape
    nt = G * SSM_CH // LANES
    gt = G // nt
    a_re, a_im = a_re.astype(F32), a_im.astype(F32)
    dt = jnp.exp(log_dt.astype(F32))[:, None]
    mag = jnp.exp(a_re * dt)
    lam_re, lam_im = mag * jnp.cos(a_im * dt), mag * jnp.sin(a_im * dt)
    den = a_re * a_re + a_im * a_im
    n_re, n_im = lam_re - 1.0, lam_im
    coef_re = (n_re * a_re + n_im * a_im) / den
    coef_im = (n_im * a_re - n_re * a_im) / den
    b_re, b_im = b_re.astype(F32), b_im.astype(F32)
    bb_re = coef_re[..., None] * b_re - coef_im[..., None] * b_im
    bb_im = coef_re[..., None] * b_im + coef_im[..., None] * b_re
    eye = jnp.eye(gt, dtype=F32)

    def in_proj(bb):
        v = bb.reshape(nt, gt, P, SSM_CH).transpose(0, 1, 3, 2)
        return jnp.einsum('igcp,gh->igchp', v, eye).reshape(nt, gt * SSM_CH, gt * P)

    def out_proj(c):
        v = c.astype(F32).reshape(nt, gt, SSM_CH, P)
        return jnp.einsum('igcp,gh->igphc', v, eye).reshape(nt, gt * P, gt * SSM_CH)

    def cmul(xr, xi, yr, yi):
        return xr * yr - xi * yi, xr * yi + xi * yr

    pows = [(jnp.ones_like(lam_re), jnp.zeros_like(lam_im))]
    for _ in range(SUBLANES):
        pows.append(cmul(*pows[-1], lam_re, lam_im))

    def in_proj_pair(j):
        pr, pi = pows[j][0][..., None], pows[j][1][..., None]
        return jnp.concatenate([in_proj(pr * bb_re - pi * bb_im), in_proj(pr * bb_im + pi * bb_re)], axis=2)

    wb = in_proj_pair(0).astype(BF16)
    wb_fir = jnp.concatenate([in_proj_pair(j) for j in range(SUBLANES)], axis=1).astype(BF16)
    wc = jnp.concatenate([out_proj(c_re), -out_proj(c_im)], axis=1).astype(BF16)
    flat = lambda x: x.reshape(1, G * P)
    rep = lambda x: jnp.broadcast_to(flat(x), (SUBLANES, G * P))
    lam = jnp.concatenate([flat(pows[1][0]), flat(pows[1][1]), jnp.zeros((SUBLANES - 2, G * P), F32)], axis=0)
    lp_re = jnp.concatenate([flat(p[0]) for p in pows[1:]], axis=0)
    lp_im = jnp.concatenate([flat(p[1]) for p in pows[1:]], axis=0)
    return dict(wb=wb, wb_fir=wb_fir, wc=wc, lam=lam, lp_re=lp_re, lp_im=lp_im,
                l8_re=rep(pows[SUBLANES][0]), l8_im=rep(pows[SUBLANES][1]))


def _s5_kernel(x_ref, g_ref, win_ref, wb_ref, wc_ref, lam_ref, lpr_ref, lpi_ref, l8r_ref, l8i_ref, d_ref, wg_ref,
               wo_ref, ir_ref, ii_ref, o_ref, fr_ref, fi_ref, sre, sim, uext, y_s, *, tc, slab, nt):
    ci = pl.program_id(1)
    hist = SUBLANES
    x = x_ref[...]
    xn = _rms(x, g_ref[...]).astype(BF16)
    uext[hist:hist + tc, :] = jnp.dot(xn, win_ref[...], preferred_element_type=F32)
    n2 = sre.shape[1] // nt

    if slab == 1:
        @pl.when(ci == 0)
        def _():
            uext[0:hist, :] = jnp.zeros((hist, uext.shape[1]), F32)
            fr_ref[...] = ir_ref[...]
            fi_ref[...] = ii_ref[...]

        for i in range(nt):
            ls = slice(i * LANES, (i + 1) * LANES)
            taps = jnp.concatenate([uext[hist - j:hist - j + tc, ls].astype(BF16) for j in range(SUBLANES)], axis=1)
            v = jnp.dot(taps, wb_ref[i], preferred_element_type=F32)
            sre[:, i * n2:(i + 1) * n2] = v[:, :n2]
            sim[:, i * n2:(i + 1) * n2] = v[:, n2:]
        uext[0:hist, :] = uext[tc:tc + hist, :]

        first = ci == 0
        nh = 2
        wh = sre.shape[1] // nh
        for hh in range(nh):
            cs = slice(hh * wh, (hh + 1) * wh)
            l8r, l8i = l8r_ref[:, cs], l8i_ref[:, cs]
            pr, pi = fr_ref[:, cs], fi_ref[:, cs]
            h0r, h0i = pr[SUBLANES - 1:SUBLANES, :], pi[SUBLANES - 1:SUBLANES, :]
            lpr, lpi = lpr_ref[:, cs], lpi_ref[:, cs]
            hr = jnp.where(first, lpr * h0r - lpi * h0i, l8r * pr - l8i * pi) + sre[0:SUBLANES, cs]
            hi = jnp.where(first, lpr * h0i + lpi * h0r, l8r * pi + l8i * pr) + sim[0:SUBLANES, cs]
            sre[0:SUBLANES, cs] = hr
            sim[0:SUBLANES, cs] = hi

            def body(k, carry, cs=cs, l8r=l8r, l8i=l8i):
                cr, cim = carry
                r0 = pl.multiple_of(k * SUBLANES, SUBLANES)
                hr = l8r * cr - l8i * cim + sre[pl.ds(r0, SUBLANES), cs]
                hi = l8r * cim + l8i * cr + sim[pl.ds(r0, SUBLANES), cs]
                sre[pl.ds(r0, SUBLANES), cs] = hr
                sim[pl.ds(r0, SUBLANES), cs] = hi
                return hr, hi

            lax.fori_loop(1, tc // SUBLANES, body, (hr, hi), unroll=2)
        fr_ref[...] = sre[tc - SUBLANES:tc, :]
        fi_ref[...] = sim[tc - SUBLANES:tc, :]
    else:
        ub = uext[hist:hist + tc, :].astype(BF16)
        for i in range(nt):
            bu = jnp.dot(ub[:, i * LANES:(i + 1) * LANES], wb_ref[i], preferred_element_type=F32)
            sre[:, i * n2:(i + 1) * n2] = bu[:, :n2]
            sim[:, i * n2:(i + 1) * n2] = bu[:, n2:]
        lr, li = lam_ref[0:1, :], lam_ref[1:2, :]
        hr, hi = ir_ref[...], ii_ref[...]
        for t in range(tc // slab):
            rs = slice(t * slab, (t + 1) * slab)
            hr, hi = lr * hr - li * hi + sre[rs, :], lr * hi + li * hr + sim[rs, :]
            sre[rs, :] = hr
            sim[rs, :] = hi
        fr_ref[...] = hr
        fi_ref[...] = hi

    for i in range(nt):
        cs = slice(i * n2, (i + 1) * n2)
        s_cat = jnp.concatenate([sre[:, cs].astype(BF16), sim[:, cs].astype(BF16)], axis=1)
        y_s[:, i * LANES:(i + 1) * LANES] = jnp.dot(s_cat, wc_ref[i], preferred_element_type=F32)
    z = jax.nn.gelu(y_s[...] + d_ref[...] * uext[hist:hist + tc, :])
    gate = jnp.dot(z.astype(BF16), wg_ref[...], preferred_element_type=F32)
    out = jnp.dot((z * jax.nn.sigmoid(gate)).astype(BF16), wo_ref[...], preferred_element_type=F32)
    o_ref[...] = x_ref[...] + out


def _s5_layer(x, init_re, init_im, g, w_in, tables, d, w_gate, w_out, *, tc, slab):
    S, R, D = x.shape
    wb = tables["wb_fir"] if slab == 1 else tables["wb"]
    wc = tables["wc"]
    nt = wb.shape[0]
    N = tables["lam"].shape[1]
    hb = init_re.shape[1]
    assert R % tc == 0 and tc % SUBLANES == 0 and (slab == 1 or R == tc)
    kern = functools.partial(_s5_kernel, tc=tc, slab=slab, nt=nt)
    const = lambda shape: pl.BlockSpec(shape, lambda s, c: (0,) * len(shape), pipeline_mode=pl.Buffered(1))
    out, fr, fi = pl.pallas_call(
        kern,
        grid=(S, R // tc),
        in_specs=[
            pl.BlockSpec((None, tc, D), lambda s, c: (s, c, 0)),
            const((1, D)),
            const((D, D)),
            const(wb.shape),
            const(wc.shape),
            const((SUBLANES, N)), const((SUBLANES, N)), const((SUBLANES, N)), const((SUBLANES, N)),
            const((SUBLANES, N)),
            const((1, D)),
            const((D, D)),
            const((D, D)),
            pl.BlockSpec((None, hb, N), lambda s, c: (s, 0, 0)),
            pl.BlockSpec((None, hb, N), lambda s, c: (s, 0, 0)),
        ],
        out_specs=[
            pl.BlockSpec((None, tc, D), lambda s, c: (s, c, 0)),
            pl.BlockSpec((None, hb, N), lambda s, c: (s, 0, 0)),
            pl.BlockSpec((None, hb, N), lambda s, c: (s, 0, 0)),
        ],
        out_shape=[jax.ShapeDtypeStruct((S, R, D), F32), jax.ShapeDtypeStruct((S, hb, N), F32),
                   jax.ShapeDtypeStruct((S, hb, N), F32)],
        scratch_shapes=[pltpu.VMEM((tc, N), F32), pltpu.VMEM((tc, N), F32),
                        pltpu.VMEM((SUBLANES + tc, D), F32), pltpu.VMEM((tc, D), F32)],
        compiler_params=_cparams(("arbitrary", "arbitrary")),
        name="s5_layer",
    )(x, g.reshape(1, D), w_in.astype(BF16), wb, wc, tables["lam"], tables["lp_re"], tables["lp_im"],
      tables["l8_re"], tables["l8_im"], d.reshape(1, D).astype(F32),
      w_gate.astype(BF16), w_out.astype(BF16), init_re, init_im)
    return out, fr, fi


D_Q = N_HEADS * HEAD_DIM
KV_DIM = N_KV_HEADS * HEAD_DIM
N_KINDS = 6
N_GATES = 3 * N_HEADS
HALF = HEAD_DIM // 2


def _rope_tables(pos):
    inv = jnp.power(ROPE_THETA, -jnp.arange(HALF, dtype=F32) / HALF)
    ang = inv[:, None] * pos.astype(F32)[None, :]
    return jnp.cos(ang), jnp.sin(ang)


def _nsa_proj_kernel(x_ref, g_ref, wt_ref, qg_ref, kg_ref, cos_ref, sin_ref, q_ref, gt_ref, *rest, tm, paged):
    if paged:
        kind_refs, (ksa_ref, vsa_ref, kw_ref, vwa_ref, kcp_ref, vcp_ref, pt_s) = rest[:N_KINDS], rest[N_KINDS:]
        n_sel = ksa_ref.shape[1] - HEAD_DIM
        key = pl.program_id(1) * tm + lax.broadcasted_iota(jnp.int32, (n_sel, tm), 1)
        expand = jnp.where(key // L_SEL == lax.broadcasted_iota(jnp.int32, (n_sel, tm), 0), 1.0, 0.0).astype(BF16)
        ones_row = jnp.where(lax.broadcasted_iota(jnp.int32, (BF16_SUBLANES, tm), 0) == 0, 1.0, 0.0).astype(BF16)
    else:
        kv_ref, pt_s = rest
    xn = _rms(x_ref[...], g_ref[...]).astype(BF16)
    pt_s[...] = lax.dot_general(wt_ref[...], xn, (((1,), (1,)), ((), ())), preferred_element_type=F32)
    cos, sin = cos_ref[...], sin_ref[...]

    def norm_rope(blk, gain):
        y = blk * lax.rsqrt(jnp.mean(blk * blk, axis=0, keepdims=True) + NORM_EPS) * gain
        x1, x2 = y[:HALF], y[HALF:]
        return x1 * cos - x2 * sin, x2 * cos + x1 * sin

    scale = HEAD_DIM ** -0.5
    for h in range(N_HEADS):
        r0 = h * HEAD_DIM
        r1, r2 = norm_rope(pt_s[r0:r0 + HEAD_DIM, :], qg_ref[...])
        q_ref[r0:r0 + HALF, :] = (r1 * scale).astype(BF16)
        q_ref[r0 + HALF:r0 + HEAD_DIM, :] = (r2 * scale).astype(BF16)
    for kind in range(N_KINDS):
        for gi in range(N_KV_HEADS):
            r0 = gi * HEAD_DIM
            blk = pt_s[D_Q + kind * KV_DIM + r0:D_Q + kind * KV_DIM + r0 + HEAD_DIM, :]
            if kind % 2 == 0:
                r1, r2 = norm_rope(blk, kg_ref[kind // 2])
                blk = jnp.concatenate([r1, r2], axis=0)
            if not paged:
                kv_ref[kind, r0:r0 + HEAD_DIM, :] = blk
                continue
            kind_refs[kind][r0:r0 + HEAD_DIM, :] = blk
            if kind < 2:
                dst = kcp_ref if kind == 0 else vcp_ref
                for p in range(tm // PAGE_SIZE):
                    dst[p, gi] = blk[:, p * PAGE_SIZE:(p + 1) * PAGE_SIZE]
            elif kind == 2:
                ksa_ref[gi, 0:n_sel, :] = expand
                ksa_ref[gi, n_sel:n_sel + HEAD_DIM, :] = blk.astype(BF16)
            elif kind == 4:
                kw_ref[gi] = blk.astype(BF16)
            else:
                dst = vsa_ref if kind == 3 else vwa_ref
                dst[gi, 0:HEAD_DIM, :] = blk.astype(BF16)
                dst[gi, HEAD_DIM:HEAD_DIM + BF16_SUBLANES, :] = ones_row
    g0 = D_Q + N_KINDS * KV_DIM
    gt_ref[...] = jax.nn.sigmoid(pt_s[g0:g0 + N_GATES, :])


def _nsa_project(x, pos, g, w_in, q_gain, k_gain, *, tm, paged):
    S, R, D = x.shape
    NP = w_in.shape[1]
    G = N_KV_HEADS
    assert R % tm == 0 and (not paged or tm % PAGE_SIZE == 0)
    nr = R // tm
    wt = w_in.T.astype(BF16)
    cos, sin = _rope_tables(pos)
    qg = jnp.broadcast_to(q_gain.astype(F32)[:, None], (HEAD_DIM, tm))
    kg = jnp.broadcast_to(k_gain.astype(F32)[:, :, None], (3, HEAD_DIM, tm))
    out_specs = [
        pl.BlockSpec((None, D_Q, tm), lambda s, i: (s, 0, i)),
        pl.BlockSpec((None, N_GATES, tm), lambda s, i: (s, 0, i)),
    ]
    out_shape = [jax.ShapeDtypeStruct((S, D_Q, R), BF16), jax.ShapeDtypeStruct((S, N_GATES, R), F32)]
    if paged:
        ppt = tm // PAGE_SIZE
        va = HEAD_DIM + BF16_SUBLANES
        grouped = lambda rows: pl.BlockSpec((None, G, rows, tm), lambda s, i: (s, 0, 0, i))
        out_specs += [pl.BlockSpec((None, KV_DIM, tm), lambda s, i: (s, 0, i))] * N_KINDS
        out_shape += [jax.ShapeDtypeStruct((S, KV_DIM, R), F32)] * N_KINDS
        for rows in (R // L_SEL + HEAD_DIM, va, HEAD_DIM, va):
            out_specs.append(grouped(rows))
            out_shape.append(jax.ShapeDtypeStruct((S, G, rows, R), BF16))
        for _ in range(2):
            out_specs.append(pl.BlockSpec((ppt, G, HEAD_DIM, PAGE_SIZE), lambda s, i: (s * nr + i, 0, 0, 0)))
            out_shape.append(jax.ShapeDtypeStruct((S * R // PAGE_SIZE, G, HEAD_DIM, PAGE_SIZE), F32))
    else:
        out_specs.append(pl.BlockSpec((None, N_KINDS, KV_DIM, tm), lambda s, i: (s, 0, 0, i)))
        out_shape.append(jax.ShapeDtypeStruct((S, N_KINDS, KV_DIM, R), F32))
    return pl.pallas_call(
        functools.partial(_nsa_proj_kernel, tm=tm, paged=paged),
        grid=(S, nr),
        in_specs=[
            pl.BlockSpec((None, tm, D), lambda s, i: (s, i, 0)),
            pl.BlockSpec((1, D), lambda s, i: (0, 0)),
            pl.BlockSpec((NP, D), lambda s, i: (0, 0)),
            pl.BlockSpec((HEAD_DIM, tm), lambda s, i: (0, 0)),
            pl.BlockSpec((3, HEAD_DIM, tm), lambda s, i: (0, 0, 0)),
            pl.BlockSpec((HALF, tm), lambda s, i: (0, i)),
            pl.BlockSpec((HALF, tm), lambda s, i: (0, i)),
        ],
        out_specs=out_specs,
        out_shape=out_shape,
        scratch_shapes=[pltpu.VMEM((NP, tm), F32)],
        compiler_params=_cparams(("arbitrary", "arbitrary")),
        name="nsa_project",
    )(x, g.reshape(1, D), wt, qg, kg, cos, sin)


PAGE_ROWS = N_KV_HEADS * HEAD_DIM
CMP_PER_PAGE = PAGE_SIZE // L_CMP


def _page_copies(pages_hbm, buf, sem, page, slot, p, by_feature):
    if not by_feature:
        return [pltpu.make_async_copy(pages_hbm.at[page], buf.at[slot, pl.ds(p * PAGE_ROWS, PAGE_ROWS), :],
                                      sem.at[slot])]
    return [pltpu.make_async_copy(pages_hbm.at[page, pl.ds(g * HEAD_DIM, HEAD_DIM), :],
                                  buf.at[slot, :, p * N_KV_HEADS + g, :], sem.at[slot])
            for g in range(N_KV_HEADS)]


def _fetch_chunk(pt_ref, pages_hbm, buf, sem, n, slot, *, nchunks, npc, by_feature):
    s = n // nchunks
    c = n % nchunks

    def start(p, carry):
        for cp in _page_copies(pages_hbm, buf, sem, pt_ref[s, c * npc + p], slot, p, by_feature):
            cp.start()
        return carry

    lax.fori_loop(0, npc, start, 0)


def _wait_chunk(pages_hbm, buf, sem, slot, *, npc, by_feature):
    def wait(p, carry):
        for cp in _page_copies(pages_hbm, buf, sem, 0, slot, p, by_feature):
            cp.wait()
        return carry

    lax.fori_loop(0, npc, wait, 0)


def _stream_pages(pt_ref, pages_hbm, buf, sem, *, nchunks, npc, total, by_feature=False):
    n = pl.program_id(0) * nchunks + pl.program_id(1)
    slot = n % 2
    kw = dict(nchunks=nchunks, npc=npc, by_feature=by_feature)

    @pl.when(n == 0)
    def _():
        _fetch_chunk(pt_ref, pages_hbm, buf, sem, n, slot, **kw)

    @pl.when(n + 1 < total)
    def _():
        _fetch_chunk(pt_ref, pages_hbm, buf, sem, n + 1, 1 - slot, **kw)

    _wait_chunk(pages_hbm, buf, sem, slot, npc=npc, by_feature=by_feature)
    return slot


def _compress_tables(w1, b1, w2):
    eye = jnp.eye(CMP_PER_PAGE, dtype=F32)
    m = jnp.einsum('lde,ck->dclke', w1.astype(F32), eye)
    m = m.reshape(HEAD_DIM * PAGE_SIZE, CMP_PER_PAGE * HEAD_DIM).astype(BF16)
    b1t = jnp.tile(b1.astype(F32), CMP_PER_PAGE).reshape(1, CMP_PER_PAGE * HEAD_DIM)
    w2bd = jnp.einsum('ef,ck->cekf', w2.astype(F32), eye)
    w2bd = w2bd.reshape(CMP_PER_PAGE * HEAD_DIM, CMP_PER_PAGE * HEAD_DIM).astype(BF16)
    return m, b1t, w2bd


def _compress_kernel(pt_ref, pages_hbm, m_ref, b1_ref, w2_ref, o_ref, buf, sem, lhs_s, *, nchunks, npc, total):
    slot = _stream_pages(pt_ref, pages_hbm, buf, sem, nchunks=nchunks, npc=npc, total=total, by_feature=True)
    for d in range(HEAD_DIM):
        lhs_s[:, d * PAGE_SIZE:(d + 1) * PAGE_SIZE] = buf[slot, d].astype(BF16)
    acc = jnp.dot(lhs_s[...], m_ref[...], preferred_element_type=F32)
    h = jax.nn.gelu(acc + b1_ref[...])
    o_ref[...] = jnp.dot(h.astype(BF16), w2_ref[...], preferred_element_type=F32)


def _compress(pages, page_table, tables, *, npc):
    S, NPG = page_table.shape
    assert NPG % npc == 0
    nchunks = NPG // npc
    m, b1t, w2bd = tables
    nrow = npc * N_KV_HEADS
    nce = CMP_PER_PAGE * HEAD_DIM
    pages2 = pages.reshape(pages.shape[0], PAGE_ROWS, PAGE_SIZE)
    kern = functools.partial(_compress_kernel, nchunks=nchunks, npc=npc, total=S * nchunks)
    out = pl.pallas_call(
        kern,
        grid_spec=pltpu.PrefetchScalarGridSpec(
            num_scalar_prefetch=1,
            grid=(S, nchunks),
            in_specs=[
                pl.BlockSpec(memory_space=pl.ANY),
                pl.BlockSpec(m.shape, lambda s, c, pt: (0, 0)),
                pl.BlockSpec(b1t.shape, lambda s, c, pt: (0, 0)),
                pl.BlockSpec(w2bd.shape, lambda s, c, pt: (0, 0)),
            ],
            out_specs=pl.BlockSpec((None, nrow, nce), lambda s, c, pt: (s, c, 0)),
            scratch_shapes=[pltpu.VMEM((2, HEAD_DIM, nrow, PAGE_SIZE), F32), pltpu.SemaphoreType.DMA((2,)),
                            pltpu.VMEM((nrow, HEAD_DIM * PAGE_SIZE), BF16)],
        ),
        out_shape=jax.ShapeDtypeStruct((S, NPG * N_KV_HEADS, nce), F32),
        compiler_params=_cparams(("arbitrary", "arbitrary")),
        name="nsa_compress",
    )(page_table, pages2, m, b1t, w2bd)
    out = out.reshape(S, NPG, N_KV_HEADS, CMP_PER_PAGE, HEAD_DIM).transpose(0, 1, 3, 2, 4)
    return out.reshape(S, NPG * CMP_PER_PAGE, N_KV_HEADS, HEAD_DIM)


HPG = N_HEADS // N_KV_HEADS
CMP_PER_SEL = L_SEL // L_CMP
NT_DIMS = (((1,), (1,)), ((), ()))


def _masked_softmax(s, mask):
    s = jnp.where(mask, s, NEG_INF)
    e = jnp.exp(s - jnp.max(s, axis=-1, keepdims=True))
    return jnp.where(mask, e / jnp.sum(e, axis=-1, keepdims=True), 0.0)


def _select_blocks(imp, pos_q, n_sel):
    nq = imp.shape[0]
    jidx = lax.broadcasted_iota(jnp.int32, (nq, n_sel), 1)
    qblk = pos_q // L_SEL
    valid = jidx * L_SEL <= pos_q
    forced = (jidx == 0) | (jidx == qblk) | (jidx == qblk - 1)
    score = jnp.where(valid & forced, FORCE_SCORE, jnp.where(valid, imp, NEG_INF))
    sct = score.T
    jrow = lax.broadcasted_iota(jnp.int32, (n_sel, nq), 0)
    sel = jnp.zeros((n_sel, nq), F32)
    for _ in range(min(TOP_N, n_sel)):
        top = jnp.max(sct, axis=0, keepdims=True)
        first = jnp.min(jnp.where(sct == top, jrow, n_sel), axis=0, keepdims=True)
        hit = jrow == first
        sct = jnp.where(hit, -jnp.inf, sct)
        sel = jnp.where(hit, 1.0, sel)
    return sel


def _cmp_block_of_column(ncol):
    c = lax.broadcasted_iota(jnp.int32, (1, ncol), 1)
    half = ncol // CMP_PER_SEL
    return jnp.where(c < half, CMP_PER_SEL * c, CMP_PER_SEL * (c - half) + 1)


Q_BLOCK = 128
GROUPS_PER_STEP = 4
BF16_SUBLANES = 2 * SUBLANES


M_FLOOR = 0.1 * NEG_INF


def _bias_softmax(s, bias):
    s = s + bias
    e = jnp.exp(s - jnp.maximum(jnp.max(s, axis=-1, keepdims=True), M_FLOOR))
    l = jnp.sum(e, axis=-1, keepdims=True)
    return e, 1.0 / jnp.where(l > 0.0, l, 1.0)


def _attn_prompt_kernel(q_ref, kc_ref, vc_ref, ksa_ref, vsa_ref, kw_ref, vwa_ref, gt_ref, o_ref, *, tk, T, gb):
    qi = pl.program_id(2)
    nq = Q_BLOCK
    nr = HPG * nq
    nc = kc_ref.shape[2]
    n_sel = nc // CMP_PER_SEL
    va = vsa_ref.shape[1]
    pos_q = qi * nq + lax.broadcasted_iota(jnp.int32, (nq, 1), 0)
    cbias = jnp.where((_cmp_block_of_column(nc) * L_CMP + (L_CMP - 1)) <= pos_q, 0.0, NEG_INF)

    def normalise(r):
        l = r[..., HEAD_DIM:HEAD_DIM + 1]
        return r[..., :HEAD_DIM] * (1.0 / jnp.where(l > 0.0, l, 1.0))

    qs, q_augs, o_cs = [], [], []
    for gl in range(gb):
        qrows = q_ref[gl].astype(F32).T
        q = jnp.concatenate([qrows[:, h * HEAD_DIM:(h + 1) * HEAD_DIM] for h in range(HPG)], axis=0)
        q = q.astype(BF16)
        s_c = jnp.dot(q, kc_ref[gl], preferred_element_type=F32).reshape(HPG, nq, nc)
        e_c, inv_c = _bias_softmax(s_c, cbias[None])
        p_c = e_c * inv_c
        o_cs.append(jnp.dot(p_c.reshape(nr, nc).astype(BF16), vc_ref[gl],
                            preferred_element_type=F32).reshape(HPG, nq, HEAD_DIM))
        imp = p_c[0]
        for h in range(1, HPG):
            imp = imp + p_c[h]
        imp = imp[:, :n_sel] + imp[:, n_sel:]
        sel = _select_blocks(imp, pos_q, n_sel).T
        unsel = jnp.where(sel > 0.5, 0.0, NEG_INF).astype(BF16)
        qs.append(q)
        q_augs.append(jnp.concatenate([jnp.concatenate([unsel] * HPG, axis=0), q], axis=1))

    def sel_tile(gl, k0, carry, causal):
        m, acc = carry
        s = jnp.dot(q_augs[gl], ksa_ref[gl, :, pl.ds(k0, tk)], preferred_element_type=F32)
        s = s.reshape(HPG, nq, tk)
        if causal:
            kpos = k0 + lax.broadcasted_iota(jnp.int32, (nq, tk), 1)
            s = s + jnp.where(kpos <= pos_q, 0.0, NEG_INF)[None]
        m_new = jnp.maximum(m, jnp.max(s, axis=-1, keepdims=True))
        p = jnp.exp(s - m_new).astype(BF16)
        pv = lax.dot_general(p.reshape(nr, tk), vsa_ref[gl, :, pl.ds(k0, tk)], NT_DIMS,
                             preferred_element_type=F32)
        return m_new, jnp.exp(m - m_new) * acc + pv.reshape(HPG, nq, va)

    def sel_tiles(k0, carries, causal):
        return tuple(sel_tile(gl, k0, carries[gl], causal) for gl in range(gb))

    last = (qi * nq + nq + tk - 1) // tk - 1
    init = (jnp.full((HPG, nq, 1), M_FLOOR, F32), jnp.zeros((HPG, nq, va), F32))
    carries = lax.fori_loop(0, last, lambda kt, c: sel_tiles(pl.multiple_of(kt * tk, tk), c, False),
                            (init,) * gb)
    carries = sel_tiles(pl.multiple_of(last * tk, tk), carries, True)

    wl = min(WINDOW + nq, T)
    w0 = pl.multiple_of(jnp.maximum(qi * nq + nq - wl, 0), LANES)
    dpos = pos_q - (w0 + lax.broadcasted_iota(jnp.int32, (nq, wl), 1))
    wbias = jnp.where((dpos >= 0) & (dpos < WINDOW), 0.0, NEG_INF)[None]
    for gl in range(gb):
        o_s = normalise(carries[gl][1])
        s_w = jnp.dot(qs[gl], kw_ref[gl, :, pl.ds(w0, wl)], preferred_element_type=F32).reshape(HPG, nq, wl)
        s_w = s_w + wbias
        e_w = jnp.exp(s_w - jnp.maximum(jnp.max(s_w, axis=-1, keepdims=True), M_FLOOR)).astype(BF16)
        o_w = normalise(lax.dot_general(e_w.reshape(nr, wl), vwa_ref[gl, :, pl.ds(w0, wl)], NT_DIMS,
                                        preferred_element_type=F32).reshape(HPG, nq, va))
        gt = gt_ref[gl]
        o_c = o_cs[gl]
        outs = []
        for h in range(HPG):
            outs.append(gt[:, h:h + 1] * o_c[h] + gt[:, HPG + h:HPG + h + 1] * o_s[h]
                        + gt[:, 2 * HPG + h:2 * HPG + h + 1] * o_w[h])
        o_ref[:, gl * HPG * HEAD_DIM:(gl + 1) * HPG * HEAD_DIM] = jnp.concatenate(outs, axis=1).astype(BF16)


def _block_expansion(n_sel, n_keys):
    return (jnp.arange(n_keys)[None, :] // L_SEL == jnp.arange(n_sel)[:, None]).astype(BF16)


def _cmp_layouts(kc, vc):
    order = jnp.concatenate([jnp.arange(0, kc.shape[1], CMP_PER_SEL), jnp.arange(1, kc.shape[1], CMP_PER_SEL)])
    kct = kc[:, order].transpose(0, 2, 3, 1).astype(BF16)
    vcr = vc[:, order].transpose(0, 2, 1, 3).astype(BF16)
    return kct, vcr


def _attend_prompt(qT, ks_aug, vs_aug, kw, vw_aug, gT, kc, vc, *, tk):
    B, _, T = qT.shape
    G = N_KV_HEADS
    nq = Q_BLOCK
    assert T % tk == 0 and T % nq == 0 and tk % L_SEL == 0
    nc = T // L_CMP
    n_sel = T // L_SEL
    kct, vcr = _cmp_layouts(kc, vc)
    q4 = qT.reshape(B, G, HPG * HEAD_DIM, T)
    gates = gT.reshape(B, 3, G, HPG, T).transpose(0, 2, 4, 1, 3).reshape(B, G, T, 3 * HPG)
    va = HEAD_DIM + BF16_SUBLANES
    gb = GROUPS_PER_STEP
    assert G % gb == 0
    once = pl.Buffered(1)
    per_group = lambda rows: pl.BlockSpec((None, gb, rows, T), lambda b, g, i: (b, g, 0, 0), pipeline_mode=once)
    return pl.pallas_call(
        functools.partial(_attn_prompt_kernel, tk=tk, T=T, gb=gb),
        grid=(B, G // gb, T // nq),
        in_specs=[
            pl.BlockSpec((None, gb, HPG * HEAD_DIM, nq), lambda b, g, i: (b, g, 0, i)),
            pl.BlockSpec((None, gb, HEAD_DIM, nc), lambda b, g, i: (b, g, 0, 0)),
            pl.BlockSpec((None, gb, nc, HEAD_DIM), lambda b, g, i: (b, g, 0, 0)),
            per_group(n_sel + HEAD_DIM), per_group(va), per_group(HEAD_DIM), per_group(va),
            pl.BlockSpec((None, gb, nq, 3 * HPG), lambda b, g, i: (b, g, i, 0)),
        ],
        out_specs=pl.BlockSpec((None, nq, gb * HPG * HEAD_DIM), lambda b, g, i: (b, i, g)),
        out_shape=jax.ShapeDtypeStruct((B, T, D_Q), BF16),
        compiler_params=_cparams(("arbitrary", "arbitrary", "arbitrary")),
        name="nsa_attend_prompt",
    )(q4, kct, vcr, ks_aug, vs_aug, kw, vw_aug, gates)


PAGE_LOOP_UNROLL = 4


def _attn_sample_kernel(pt_ref, q_ref, kc_ref, vc_ref, kwc_ref, vwc_ref, kwn_ref, vwn_ref, ksn_ref, vsn_ref,
                        gt_ref, e_ref, dm_ref, ks_hbm, vs_hbm, o_ref,
                        kbuf, vbuf, ksem, vsem, s_s, sel_s, m_s, l_s, acc_s, oc_s, ow_s,
                        *, nchunks, npc, total, past, tq):
    c = pl.program_id(1)
    kslot = _stream_pages(pt_ref, ks_hbm, kbuf, ksem, nchunks=nchunks, npc=npc, total=total)
    vslot = _stream_pages(pt_ref, vs_hbm, vbuf, vsem, nchunks=nchunks, npc=npc, total=total)
    q = q_ref[...]
    nr = q.shape[0]
    ngt = N_KV_HEADS * tq
    pos_q = past + lax.broadcasted_iota(jnp.int32, (nr, 1), 0) % tq
    n_sel = sel_s.shape[1]
    sel_past = past // L_SEL

    def fold(o_full):
        o = o_full * dm_ref[...]
        out = o[:, 0:HEAD_DIM]
        for gi in range(1, N_KV_HEADS):
            out = out + o[:, gi * HEAD_DIM:(gi + 1) * HEAD_DIM]
        return out

    def nt(p, vt):
        return lax.dot_general(p.astype(BF16), vt.astype(BF16), NT_DIMS, preferred_element_type=F32)

    @pl.when(c == 0)
    def _():
        nc = kc_ref.shape[1]
        s_c = jnp.dot(q, kc_ref[...], preferred_element_type=F32)
        cmask = (_cmp_block_of_column(nc) * L_CMP + (L_CMP - 1)) <= pos_q
        p_c = _masked_softmax(s_c, cmask)
        oc_s[...] = fold(nt(p_c, vc_ref[...]))
        imp = p_c[0:ngt]
        for h in range(1, HPG):
            imp = imp + p_c[h * ngt:(h + 1) * ngt]
        imp = imp[:, :n_sel] + imp[:, n_sel:]
        pad = LANES - ngt
        imp = jnp.concatenate([imp, jnp.zeros((pad, n_sel), F32)], axis=0)
        pos_pad = past + lax.broadcasted_iota(jnp.int32, (LANES, 1), 0) % tq
        sel = _select_blocks(imp, pos_pad, n_sel).T[0:ngt]
        sel_s[...] = jnp.concatenate([sel] * HPG, axis=0)

        nwc = kwc_ref.shape[1]
        nwn = kwn_ref.shape[1]
        s_w = jnp.concatenate([jnp.dot(q, kwc_ref[...].astype(BF16), preferred_element_type=F32),
                               jnp.dot(q, kwn_ref[...].astype(BF16), preferred_element_type=F32)], axis=1)
        lane = lax.broadcasted_iota(jnp.int32, (nr, nwc + nwn), 1)
        pos_w = jnp.where(lane < nwc, past - nwc + lane, past + lane - nwc)
        dpos = pos_q - pos_w
        p_w = _masked_softmax(s_w, (dpos >= 0) & (dpos < WINDOW) & (pos_w >= 0))
        ow_s[...] = fold(nt(p_w[:, :nwc], vwc_ref[...]) + nt(p_w[:, nwc:], vwn_ref[...]))

        nsn = ksn_ref.shape[1]
        s_n = jnp.dot(q, ksn_ref[...].astype(BF16), preferred_element_type=F32)
        kpos = past + lax.broadcasted_iota(jnp.int32, (nr, nsn), 1)
        msk = (sel_s[:, sel_past:sel_past + 1] > 0.5) & (kpos <= pos_q)
        s_n = jnp.where(msk, s_n, NEG_INF)
        m0 = jnp.max(s_n, axis=-1, keepdims=True)
        p_n = jnp.where(msk, jnp.exp(s_n - m0), 0.0)
        m_s[...] = m0
        l_s[...] = jnp.sum(p_n, axis=-1, keepdims=True)
        acc_s[...] = nt(p_n, vsn_ref[...])

    def qk(p, carry):
        r0 = pl.multiple_of(p * PAGE_ROWS, PAGE_ROWS)
        c0 = pl.multiple_of(p * PAGE_SIZE, PAGE_SIZE)
        s_s[:, pl.ds(c0, PAGE_SIZE)] = jnp.dot(q, kbuf[kslot, pl.ds(r0, PAGE_ROWS), :].astype(BF16),
                                                preferred_element_type=F32)
        return carry

    lax.fori_loop(0, npc, qk, 0, unroll=PAGE_LOOP_UNROLL)
    nk = npc * PAGE_SIZE
    sel_c = sel_s[:, pl.ds(pl.multiple_of(c * (nk // L_SEL), LANES), nk // L_SEL)]
    chosen = jnp.dot(sel_c.astype(BF16), e_ref[...], preferred_element_type=F32)
    kpos = c * nk + lax.broadcasted_iota(jnp.int32, (nr, nk), 1)
    msk = (chosen > 0.5) & (kpos <= pos_q)
    s = jnp.where(msk, s_s[...], NEG_INF)
    m_old = m_s[...]
    m_new = jnp.maximum(m_old, jnp.max(s, axis=-1, keepdims=True))
    alpha = jnp.exp(m_old - m_new)
    p = jnp.where(msk, jnp.exp(s - m_new), 0.0)
    l_s[...] = alpha * l_s[...] + jnp.sum(p, axis=-1, keepdims=True)
    m_s[...] = m_new
    s_s[...] = p

    def pv(p_, acc):
        r0 = pl.multiple_of(p_ * PAGE_ROWS, PAGE_ROWS)
        c0 = pl.multiple_of(p_ * PAGE_SIZE, PAGE_SIZE)
        return acc + nt(s_s[:, pl.ds(c0, PAGE_SIZE)], vbuf[vslot, pl.ds(r0, PAGE_ROWS), :])

    acc = lax.fori_loop(0, npc, pv, jnp.zeros((nr, PAGE_ROWS), F32), unroll=PAGE_LOOP_UNROLL)
    acc_s[...] = alpha * acc_s[...] + acc

    @pl.when(c == nchunks - 1)
    def _():
        l = l_s[...]
        o_sel = fold(acc_s[...]) / jnp.where(l > 0.0, l, 1.0)
        gt = gt_ref[...]
        o_ref[...] = gt[:, 0:1] * oc_s[...] + gt[:, 1:2] * o_sel + gt[:, 2:3] * ow_s[...]


def _attend_sample(qT, kvT, gT, kc_all, vc_all, win_k, win_v, pool_ks, pool_vs, page_table, *, npc):
    Bs, NPG = page_table.shape
    tq = qT.shape[1] // Bs
    G = N_KV_HEADS
    past = NPG * PAGE_SIZE
    nchunks = NPG // npc
    nk = npc * PAGE_SIZE
    assert NPG % npc == 0 and nk // L_SEL == LANES
    nr = HPG * G * tq
    eye = jnp.eye(G, dtype=qT.dtype)
    q5 = qT.reshape(G, HPG, HEAD_DIM, tq, Bs).transpose(4, 1, 0, 3, 2)
    q_bd = jnp.einsum('bhgtd,gk->bhgtkd', q5, eye).reshape(Bs, nr, G * HEAD_DIM)
    dm = jnp.broadcast_to(jnp.eye(G, dtype=F32)[None, :, None, :, None],
                          (HPG, G, tq, G, HEAD_DIM)).reshape(nr, G * HEAD_DIM)
    gates = gT.reshape(3, G, HPG, tq, Bs).transpose(4, 2, 1, 3, 0).reshape(Bs, nr, 3)

    def new_rows(kind):
        r = kvT[kind].reshape(G * HEAD_DIM, tq, Bs).transpose(2, 0, 1)
        return jnp.pad(r, ((0, 0), (0, 0), (0, LANES - tq)))

    nc = kc_all.shape[1]
    half = nc // CMP_PER_SEL
    n_sel = -(-(half) // LANES) * LANES
    assert past // L_SEL < n_sel

    def cmp_fm(x):
        padw = ((0, 0), (0, 0), (0, n_sel - half))
        parts = [jnp.pad(x[:, par::CMP_PER_SEL].transpose(0, 2, 3, 1).reshape(Bs, G * HEAD_DIM, half), padw)
                 for par in range(CMP_PER_SEL)]
        return jnp.concatenate(parts, axis=2).astype(BF16)

    n_win = win_k.shape[-1]
    wk = win_k.reshape(Bs, G * HEAD_DIM, n_win)
    wv = win_v.reshape(Bs, G * HEAD_DIM, n_win)
    e = _block_expansion(nk // L_SEL, nk)
    ks2 = pool_ks.reshape(pool_ks.shape[0], PAGE_ROWS, PAGE_SIZE)
    vs2 = pool_vs.reshape(pool_vs.shape[0], PAGE_ROWS, PAGE_SIZE)
    per_b = lambda shape: pl.BlockSpec((None,) + shape, lambda b, c, pt: (b, 0, 0))
    kern = functools.partial(_attn_sample_kernel, nchunks=nchunks, npc=npc, total=Bs * nchunks, past=past, tq=tq)
    o = pl.pallas_call(
        kern,
        grid_spec=pltpu.PrefetchScalarGridSpec(
            num_scalar_prefetch=1,
            grid=(Bs, nchunks),
            in_specs=[
                per_b((nr, G * HEAD_DIM)),
                per_b((G * HEAD_DIM, 2 * n_sel)), per_b((G * HEAD_DIM, 2 * n_sel)),
                per_b((G * HEAD_DIM, n_win)), per_b((G * HEAD_DIM, n_win)),
                per_b((G * HEAD_DIM, LANES)), per_b((G * HEAD_DIM, LANES)),
                per_b((G * HEAD_DIM, LANES)), per_b((G * HEAD_DIM, LANES)),
                per_b((nr, 3)),
                pl.BlockSpec(e.shape, lambda b, c, pt: (0, 0)),
                pl.BlockSpec(dm.shape, lambda b, c, pt: (0, 0)),
                pl.BlockSpec(memory_space=pl.ANY),
                pl.BlockSpec(memory_space=pl.ANY),
            ],
            out_specs=pl.BlockSpec((None, nr, HEAD_DIM), lambda b, c, pt: (b, 0, 0)),
            scratch_shapes=[
                pltpu.VMEM((2, npc * PAGE_ROWS, PAGE_SIZE), F32), pltpu.VMEM((2, npc * PAGE_ROWS, PAGE_SIZE), F32),
                pltpu.SemaphoreType.DMA((2,)), pltpu.SemaphoreType.DMA((2,)),
                pltpu.VMEM((nr, nk), F32), pltpu.VMEM((nr, n_sel), F32),
                pltpu.VMEM((nr, 1), F32), pltpu.VMEM((nr, 1), F32), pltpu.VMEM((nr, G * HEAD_DIM), F32),
                pltpu.VMEM((nr, HEAD_DIM), F32), pltpu.VMEM((nr, HEAD_DIM), F32),
            ],
        ),
        out_shape=jax.ShapeDtypeStruct((Bs, nr, HEAD_DIM), F32),
        compiler_params=_cparams(("arbitrary", "arbitrary")),
        name="nsa_attend_sample",
    )(page_table, q_bd, cmp_fm(kc_all), cmp_fm(vc_all), wk, wv, new_rows(4), new_rows(5), new_rows(2), new_rows(3),
      gates, e, dm, ks2, vs2)
    o = o.reshape(Bs, HPG, G, tq, HEAD_DIM).transpose(3, 0, 2, 1, 4)
    return o.reshape(tq * Bs, D_Q).astype(BF16)


def _out_proj_kernel(a_ref, w_ref, x_ref, o_ref):
    o_ref[...] = x_ref[...] + jnp.dot(a_ref[...], w_ref[...], preferred_element_type=F32)


def _out_proj(a, w, x, *, tm):
    R, Kd = a.shape
    N = w.shape[1]
    assert R % tm == 0
    return pl.pallas_call(
        _out_proj_kernel,
        grid=(R // tm,),
        in_specs=[pl.BlockSpec((tm, Kd), lambda i: (i, 0)), pl.BlockSpec((Kd, N), lambda i: (0, 0)),
                  pl.BlockSpec((tm, N), lambda i: (i, 0))],
        out_specs=pl.BlockSpec((tm, N), lambda i: (i, 0)),
        out_shape=jax.ShapeDtypeStruct((R, N), F32),
        compiler_params=_cparams(("arbitrary",)),
        name="out_proj",
    )(a, w.astype(BF16), x)


PROMPT_ROW_TILE = 512
S5_CHUNK = 256
SEL_KEY_TILE = 1024
SAMPLE_PAGES_PER_CHUNK = 64


def _feature_major(cache):
    nd = cache.ndim
    return jnp.moveaxis(cache, nd - 3, nd - 1)


def _nsa_prompt_layer(x, g, w_in, q_gain, k_gain, ck, cv, w_out):
    B, T, D = x.shape
    G = N_KV_HEADS
    outs = _nsa_project(x, jnp.arange(T), g, w_in, q_gain, k_gain, tm=PROMPT_ROW_TILE, paged=True)
    qT, gT = outs[:2]
    kinds = outs[2:2 + N_KINDS]
    ks_aug, vs_aug, kw, vw_aug, kcp, vcp = outs[2 + N_KINDS:]
    npg = T // PAGE_SIZE
    ptab = jnp.arange(B * npg, dtype=jnp.int32).reshape(B, npg)
    kc = _compress(kcp, ptab, _compress_tables(*ck), npc=npg)
    vc = _compress(vcp, ptab, _compress_tables(*cv), npc=npg)
    o = _attend_prompt(qT, ks_aug, vs_aug, kw, vw_aug, gT, kc, vc, tk=SEL_KEY_TILE)
    y = _out_proj(o.reshape(B * T, D_Q), w_out, x.reshape(B * T, D), tm=PROMPT_ROW_TILE).reshape(B, T, D)
    rows = [k.reshape(B, G, HEAD_DIM, T).transpose(0, 3, 1, 2) for k in kinds]
    n_keep = min(WINDOW, T)
    rows[4] = rows[4][:, T - n_keep:]
    rows[5] = rows[5][:, T - n_keep:]
    return y, rows


def _nsa_sample_layer(x_tm, Bs, pools, win_k, win_v, page_table, g, w_in, q_gain, k_gain, ck, cv, w_out):
    _, R, D = x_tm.shape
    G = N_KV_HEADS
    tq = R // Bs
    npg = page_table.shape[1]
    past = npg * PAGE_SIZE
    pos = past + jnp.arange(R) // Bs
    qT, gT, kvT = _nsa_project(x_tm, pos, g, w_in, q_gain, k_gain, tm=R, paged=False)
    qT, kvT, gT = qT[0], kvT[0], gT[0]
    new_pad = -(-tq // L_SEL) * L_SEL
    assert new_pad <= PAGE_SIZE
    new_tab = jnp.arange(Bs, dtype=jnp.int32).reshape(1, Bs)

    def cmp_all(pool, kind, tabs):
        past_blocks = _compress(_feature_major(pool), page_table, tabs, npc=SAMPLE_PAGES_PER_CHUNK)
        page = kvT[kind].reshape(G, HEAD_DIM, tq, Bs).transpose(3, 0, 1, 2)
        page = jnp.pad(page, ((0, 0), (0, 0), (0, 0), (0, PAGE_SIZE - tq)))
        new_blocks = _compress(page, new_tab, tabs, npc=Bs).reshape(Bs, CMP_PER_PAGE, G, HEAD_DIM)
        return jnp.concatenate([past_blocks, new_blocks[:, :new_pad // L_CMP]], axis=1)

    kc_all = cmp_all(pools[0], 0, _compress_tables(*ck))
    vc_all = cmp_all(pools[1], 1, _compress_tables(*cv))
    o = _attend_sample(qT, kvT, gT, kc_all, vc_all, _feature_major(win_k), _feature_major(win_v),
                       _feature_major(pools[2]), _feature_major(pools[3]), page_table,
                       npc=SAMPLE_PAGES_PER_CHUNK)
    y = _out_proj(o, w_out, x_tm[0], tm=R)[None]
    rows = [kvT[k].reshape(G, HEAD_DIM, tq, Bs).transpose(3, 2, 0, 1) for k in range(N_KINDS)]
    return y, rows


def kernel(x_prompt, x_sample, cache_k_cmp, cache_v_cmp, cache_k_slc, cache_v_slc, cache_k_win, cache_v_win,
           state_ssm_re, state_ssm_im, state_conv, page_table, norm_mix, norm_ffn,
           ssm_w_in, ssm_a_re, ssm_a_im, ssm_log_dt, ssm_b_re, ssm_b_im, ssm_c_re, ssm_c_im,
           ssm_d, ssm_w_gate, ssm_w_out,
           nsa_w_in, nsa_q_gain, nsa_k_gain, nsa_ck_w1, nsa_ck_b1, nsa_ck_w2,
           nsa_cv_w1, nsa_cv_b1, nsa_cv_w2, nsa_w_out,
           ffn_w_up, ffn_conv_w, ffn_conv_b, ffn_w_down):
    B, T, D = x_prompt.shape
    Bs, Tq, _ = x_sample.shape
    depth = norm_mix.shape[0]
    Fd = ffn_w_down.shape[1]
    G, P = ssm_a_re.shape[1:]
    xp = x_prompt
    xs = x_sample.transpose(1, 0, 2).reshape(1, Tq * Bs, D)
    p_sre, p_sim, s_sre, s_sim, p_cv, s_cv = [], [], [], [], [], []
    p_rows = [[] for _ in range(N_KINDS)]
    s_rows = [[] for _ in range(N_KINDS)]
    for i in range(depth):
        li = i // 2
        if i % 2 == 0:
            tabs = _s5_tables(ssm_a_re[li], ssm_a_im[li], ssm_log_dt[li], ssm_b_re[li], ssm_b_im[li],
                              ssm_c_re[li], ssm_c_im[li])
            sp = (norm_mix[i], ssm_w_in[li], tabs, ssm_d[li], ssm_w_gate[li], ssm_w_out[li])
            z = jnp.zeros((B, SUBLANES, G * P), F32)
            xp, fr, fi = _s5_layer(xp, z, z, *sp, tc=min(S5_CHUNK, T), slab=1)
            p_sre.append(fr[:, SUBLANES - 1].reshape(B, G, P))
            p_sim.append(fi[:, SUBLANES - 1].reshape(B, G, P))
            xs, fr, fi = _s5_layer(xs, state_ssm_re[li].reshape(1, Bs, G * P).astype(F32),
                                   state_ssm_im[li].reshape(1, Bs, G * P).astype(F32), *sp, tc=Tq * Bs, slab=Bs)
            s_sre.append(fr.reshape(Bs, G, P))
            s_sim.append(fi.reshape(Bs, G, P))
        else:
            ap = (norm_mix[i], nsa_w_in[li], nsa_q_gain[li], nsa_k_gain[li],
                  (nsa_ck_w1[li], nsa_ck_b1[li], nsa_ck_w2[li]), (nsa_cv_w1[li], nsa_cv_b1[li], nsa_cv_w2[li]),
                  nsa_w_out[li])
            xp, rp = _nsa_prompt_layer(xp, *ap)
            xs, rs = _nsa_sample_layer(xs, Bs, (cache_k_cmp[li], cache_v_cmp[li], cache_k_slc[li], cache_v_slc[li]),
                                       cache_k_win[li], cache_v_win[li], page_table, *ap)
            for j in range(N_KINDS):
                p_rows[j].append(rp[j])
                s_rows[j].append(rs[j])
        fw = (norm_ffn[i], ffn_w_up[i], ffn_conv_w[i], ffn_conv_b[i], ffn_w_down[i])
        xp, cp = _conv_ffn(xp, jnp.zeros((B, SUBLANES, Fd), F32), *fw, shift=1, tm=min(PROMPT_ROW_TILE, T))
        p_cv.append(cp[:, SUBLANES - (CONV_W - 1):])
        init = state_conv[i].astype(F32).transpose(1, 0, 2).reshape(1, (CONV_W - 1) * Bs, Fd)
        xs, cs = _conv_ffn(xs, init, *fw, shift=Bs, tm=Tq * Bs)
        s_cv.append(cs.reshape(CONV_W - 1, Bs, Fd).transpose(1, 0, 2))
    st = jnp.stack
    y_sample = xs.reshape(Tq, Bs, D).transpose(1, 0, 2)
    return (xp, y_sample,
            st(p_sre), st(p_sim), *[st(r) for r in p_rows], st(p_cv),
            st(s_sre), st(s_sim), *[st(r) for r in s_rows], st(s_cv))
```

```python
import functools
import math

import jax
import jax.numpy as jnp
from jax import lax
from jax.experimental import pallas as pl
from jax.experimental.pallas import tpu as pltpu

F32 = jnp.float32
BF16 = jnp.bfloat16

NORM_EPS = 1e-6
SSM_CH = 16
SSM_STATE = 64
N_HEADS = 16
N_KV_HEADS = 4
HEAD_DIM = 64
L_CMP = 32
L_SEL = 64
TOP_N = 16
WINDOW = 512
PAGE_SIZE = 128
ROPE_THETA = 10000.0
NEG_INF = -1e30
FORCE_SCORE = 1e4
CONV_W = 3

LANES = 128
SUBLANES = 8
VMEM_LIMIT = 56 * 1024 * 1024


def _cparams(sem):
    return pltpu.CompilerParams(dimension_semantics=sem, vmem_limit_bytes=VMEM_LIMIT)


def _rms(x, g):
    return x * lax.rsqrt(jnp.mean(x * x, axis=-1, keepdims=True) + NORM_EPS) * g


FFN_MIN_SUBTILE = 256
FFN_CHUNKS = 1


def _ffn_kernel(x_ref, g_ref, wa_ref, wb_ref, cw_ref, cb_ref, wd_ref, init_ref,
                o_ref, buf_ref, xn_s, acc_s, aext_s, carry_s, *, shift, base, tm, nf, nsub):
    i = pl.program_id(1)
    j = pl.program_id(2)

    @pl.when(j == 0)
    def _():
        x = x_ref[...]
        xn_s[...] = _rms(x, g_ref[...]).astype(BF16)
        acc_s[...] = x

    @pl.when(i == 0)
    def _():
        aext_s[0:base, :] = init_ref[...]

    @pl.when(i > 0)
    def _():
        aext_s[0:base, :] = carry_s[j]

    cw = cw_ref[...]
    ts = tm // nsub
    for sub in range(nsub):
        r0 = sub * ts
        xn = xn_s[r0:r0 + ts, :]
        a = jnp.dot(xn, wa_ref[...], preferred_element_type=F32)
        b = jnp.dot(xn, wb_ref[...], preferred_element_type=F32)
        aext_s[base + r0:base + r0 + ts, :] = a
        a1 = aext_s[base - shift + r0:base - shift + r0 + ts, :]
        a2 = aext_s[base - 2 * shift + r0:base - 2 * shift + r0 + ts, :]
        c = cb_ref[...] + cw[0:1, :] * a2
        c = c + cw[1:2, :] * a1
        c = c + cw[2:3, :] * a
        h = (jax.nn.silu(c) * b).astype(BF16)
        acc_s[r0:r0 + ts, :] += jnp.dot(h, wd_ref[...], preferred_element_type=F32)
    tail = aext_s[tm:tm + base, :]
    carry_s[j] = tail
    buf_ref[...] = tail

    @pl.when(j == nf - 1)
    def _():
        o_ref[...] = acc_s[...]


def _conv_ffn(x, init, g, w_up, conv_w, conv_b, w_down, *, shift, tm):
    S, R, D = x.shape
    Fd = w_down.shape[0]
    base = init.shape[1]
    nf = FFN_CHUNKS
    tf = Fd // nf
    assert tf * nf == Fd and tf % LANES == 0 and R % tm == 0 and base >= 2 * shift
    wu = w_up.astype(BF16)
    wd = w_down.astype(BF16)
    nsub = 2 if tm % (2 * FFN_MIN_SUBTILE) == 0 else 1
    kern = functools.partial(_ffn_kernel, shift=shift, base=base, tm=tm, nf=nf, nsub=nsub)
    wmode = pl.Buffered(1) if nf == 1 else None
    out, buf = pl.pallas_call(
        kern,
        grid=(S, R // tm, nf),
        in_specs=[
            pl.BlockSpec((None, tm, D), lambda s, i, j: (s, i, 0)),
            pl.BlockSpec((1, D), lambda s, i, j: (0, 0)),
            pl.BlockSpec((D, tf), lambda s, i, j: (0, j), pipeline_mode=wmode),
            pl.BlockSpec((D, tf), lambda s, i, j: (0, nf + j), pipeline_mode=wmode),
            pl.BlockSpec((CONV_W, tf), lambda s, i, j: (0, j)),
            pl.BlockSpec((1, tf), lambda s, i, j: (0, j)),
            pl.BlockSpec((tf, D), lambda s, i, j: (j, 0), pipeline_mode=wmode),
            pl.BlockSpec((None, base, tf), lambda s, i, j: (s, 0, j)),
        ],
        out_specs=[
            pl.BlockSpec((None, tm, D), lambda s, i, j: (s, i, 0)),
            pl.BlockSpec((None, None, base, tf), lambda s, i, j: (s, i, 0, j)),
        ],
        out_shape=[jax.ShapeDtypeStruct((S, R, D), F32), jax.ShapeDtypeStruct((S, R // tm, base, Fd), F32)],
        scratch_shapes=[
            pltpu.VMEM((tm, D), BF16),
            pltpu.VMEM((tm, D), F32),
            pltpu.VMEM((tm + base, tf), F32),
            pltpu.VMEM((nf, base, tf), F32),
        ],
        compiler_params=_cparams(("arbitrary", "arbitrary", "arbitrary")),
        name="conv_ffn",
    )(x, g.reshape(1, D), wu, wu, conv_w, conv_b.reshape(1, Fd), wd, init)
    return out, buf[:, R // tm - 1]


def _s5_tables(a_re, a_im, log_dt, b_re, b_im, c_re, c_im):
    G, P = a_re.shape
    nt = G * SSM_CH // LANES
    gt = G // nt
    a_re, a_im = a_re.astype(F32), a_im.astype(F32)
    dt = jnp.exp(log_dt.astype(F32))[:, None]
    mag = jnp.exp(a_re * dt)
    lam_re, lam_im = mag * jnp.cos(a_im * dt), mag * jnp.sin(a_im * dt)
    den = a_re * a_re + a_im * a_im
    n_re, n_im = lam_re - 1.0, lam_im
    coef_re = (n_re * a_re + n_im * a_im) / den
    coef_im = (n_im * a_re - n_re * a_im) / den
    b_re, b_im = b_re.astype(F32), b_im.astype(F32)
    bb_re = coef_re[..., None] * b_re - coef_im[..., None] * b_im
    bb_im = coef_re[..., None] * b_im + coef_im[..., None] * b_re
    eye = jnp.eye(gt, dtype=F32)

    def in_proj(bb):
        v = bb.reshape(nt, gt, P, SSM_CH).transpose(0, 1, 3, 2)
        return jnp.einsum('igcp,gh->igchp', v, eye).reshape(nt, gt * SSM_CH, gt * P)

    def out_proj(c):
        v = c.astype(F32).reshape(nt, gt, SSM_CH, P)
        return jnp.einsum('igcp,gh->igphc', v, eye).reshape(nt, gt * P, gt * SSM_CH)

    def cmul(xr, xi, yr, yi):
        return xr * yr - xi * yi, xr * yi + xi * yr

    pows = [(jnp.ones_like(lam_re), jnp.zeros_like(lam_im))]
    for _ in range(SUBLANES):
        pows.append(cmul(*pows[-1], lam_re, lam_im))

    def in_proj_pair(j):
        pr, pi = pows[j][0][..., None], pows[j][1][..., None]
        return jnp.concatenate([in_proj(pr * bb_re - pi * bb_im), in_proj(pr * bb_im + pi * bb_re)], axis=2)

    wb = in_proj_pair(0).astype(BF16)
    wb_fir = jnp.concatenate([in_proj_pair(j) for j in range(SUBLANES)], axis=1).astype(BF16)
    wc = jnp.concatenate([out_proj(c_re), -out_proj(c_im)], axis=1).astype(BF16)
    flat = lambda x: x.reshape(1, G * P)
    rep = lambda x: jnp.broadcast_to(flat(x), (SUBLANES, G * P))
    lam = jnp.concatenate([flat(pows[1][0]), flat(pows[1][1]), jnp.zeros((SUBLANES - 2, G * P), F32)], axis=0)
    lp_re = jnp.concatenate([flat(p[0]) for p in pows[1:]], axis=0)
    lp_im = jnp.concatenate([flat(p[1]) for p in pows[1:]], axis=0)
    return dict(wb=wb, wb_fir=wb_fir, wc=wc, lam=lam, lp_re=lp_re, lp_im=lp_im,
                l8_re=rep(pows[SUBLANES][0]), l8_im=rep(pows[SUBLANES][1]))


def _s5_kernel(x_ref, g_ref, win_ref, wb_ref, wc_ref, lam_ref, lpr_ref, lpi_ref, l8r_ref, l8i_ref, d_ref, wg_ref,
               wo_ref, ir_ref, ii_ref, o_ref, fr_ref, fi_ref, sre, sim, uext, y_s, *, tc, slab, nt):
    ci = pl.program_id(1)
    hist = SUBLANES
    x = x_ref[...]
    xn = _rms(x, g_ref[...]).astype(BF16)
    uext[hist:hist + tc, :] = jnp.dot(xn, win_ref[...], preferred_element_type=F32)
    n2 = sre.shape[1] // nt

    if slab == 1:
        @pl.when(ci == 0)
        def _():
            uext[0:hist, :] = jnp.zeros((hist, uext.shape[1]), F32)
            fr_ref[...] = ir_ref[...]
            fi_ref[...] = ii_ref[...]

        first = ci == 0
        for i in range(nt):
            ls = slice(i * LANES, (i + 1) * LANES)
            cs = slice(i * n2, (i + 1) * n2)
            taps = jnp.concatenate([uext[hist - j:hist - j + tc, ls].astype(BF16) for j in range(SUBLANES)], axis=1)
            v = jnp.dot(taps, wb_ref[i], preferred_element_type=F32)
            l8r, l8i = l8r_ref[:, cs], l8i_ref[:, cs]
            pr, pi = fr_ref[:, cs], fi_ref[:, cs]
            h0r, h0i = pr[SUBLANES - 1:SUBLANES, :], pi[SUBLANES - 1:SUBLANES, :]
            lpr, lpi = lpr_ref[:, cs], lpi_ref[:, cs]
            hr = jnp.where(first, lpr * h0r - lpi * h0i, l8r * pr - l8i * pi) + v[0:SUBLANES, :n2]
            hi = jnp.where(first, lpr * h0i + lpi * h0r, l8r * pi + l8i * pr) + v[0:SUBLANES, n2:]
            slabs_r, slabs_i = [hr], [hi]
            for k in range(1, tc // SUBLANES):
                rs = slice(k * SUBLANES, (k + 1) * SUBLANES)
                hr, hi = l8r * hr - l8i * hi + v[rs, :n2], l8r * hi + l8i * hr + v[rs, n2:]
                slabs_r.append(hr)
                slabs_i.append(hi)
            sre[:, cs] = jnp.concatenate(slabs_r, axis=0)
            sim[:, cs] = jnp.concatenate(slabs_i, axis=0)
        uext[0:hist, :] = uext[tc:tc + hist, :]
        fr_ref[...] = sre[tc - SUBLANES:tc, :]
        fi_ref[...] = sim[tc - SUBLANES:tc, :]
    else:
        ub = uext[hist:hist + tc, :].astype(BF16)
        for i in range(nt):
            bu = jnp.dot(ub[:, i * LANES:(i + 1) * LANES], wb_ref[i], preferred_element_type=F32)
            sre[:, i * n2:(i + 1) * n2] = bu[:, :n2]
            sim[:, i * n2:(i + 1) * n2] = bu[:, n2:]
        lr, li = lam_ref[0:1, :], lam_ref[1:2, :]
        hr, hi = ir_ref[...], ii_ref[...]
        for t in range(tc // slab):
            rs = slice(t * slab, (t + 1) * slab)
            hr, hi = lr * hr - li * hi + sre[rs, :], lr * hi + li * hr + sim[rs, :]
            sre[rs, :] = hr
            sim[rs, :] = hi
        fr_ref[...] = hr
        fi_ref[...] = hi

    for i in range(nt):
        cs = slice(i * n2, (i + 1) * n2)
        s_cat = jnp.concatenate([sre[:, cs].astype(BF16), sim[:, cs].astype(BF16)], axis=1)
        y_s[:, i * LANES:(i + 1) * LANES] = jnp.dot(s_cat, wc_ref[i], preferred_element_type=F32)
    z = jax.nn.gelu(y_s[...] + d_ref[...] * uext[hist:hist + tc, :])
    gate = jnp.dot(z.astype(BF16), wg_ref[...], preferred_element_type=F32)
    out = jnp.dot((z * jax.nn.sigmoid(gate)).astype(BF16), wo_ref[...], preferred_element_type=F32)
    o_ref[...] = x_ref[...] + out


def _s5_layer(x, init_re, init_im, g, w_in, tables, d, w_gate, w_out, *, tc, slab):
    S, R, D = x.shape
    wb = tables["wb_fir"] if slab == 1 else tables["wb"]
    wc = tables["wc"]
    nt = wb.shape[0]
    N = tables["lam"].shape[1]
    hb = init_re.shape[1]
    assert R % tc == 0 and tc % SUBLANES == 0 and (slab == 1 or R == tc)
    kern = functools.partial(_s5_kernel, tc=tc, slab=slab, nt=nt)
    const = lambda shape: pl.BlockSpec(shape, lambda s, c: (0,) * len(shape), pipeline_mode=pl.Buffered(1))
    out, fr, fi = pl.pallas_call(
        kern,
        grid=(S, R // tc),
        in_specs=[
            pl.BlockSpec((None, tc, D), lambda s, c: (s, c, 0)),
            const((1, D)),
            const((D, D)),
            const(wb.shape),
            const(wc.shape),
            const((SUBLANES, N)), const((SUBLANES, N)), const((SUBLANES, N)), const((SUBLANES, N)),
            const((SUBLANES, N)),
            const((1, D)),
            const((D, D)),
            const((D, D)),
            pl.BlockSpec((None, hb, N), lambda s, c: (s, 0, 0)),
            pl.BlockSpec((None, hb, N), lambda s, c: (s, 0, 0)),
        ],
        out_specs=[
            pl.BlockSpec((None, tc, D), lambda s, c: (s, c, 0)),
            pl.BlockSpec((None, hb, N), lambda s, c: (s, 0, 0)),
            pl.BlockSpec((None, hb, N), lambda s, c: (s, 0, 0)),
        ],
        out_shape=[jax.ShapeDtypeStruct((S, R, D), F32), jax.ShapeDtypeStruct((S, hb, N), F32),
                   jax.ShapeDtypeStruct((S, hb, N), F32)],
        scratch_shapes=[pltpu.VMEM((tc, N), F32), pltpu.VMEM((tc, N), F32),
                        pltpu.VMEM((SUBLANES + tc, D), F32), pltpu.VMEM((tc, D), F32)],
        compiler_params=_cparams(("arbitrary", "arbitrary")),
        name="s5_layer",
    )(x, g.reshape(1, D), w_in.astype(BF16), wb, wc, tables["lam"], tables["lp_re"], tables["lp_im"],
      tables["l8_re"], tables["l8_im"], d.reshape(1, D).astype(F32),
      w_gate.astype(BF16), w_out.astype(BF16), init_re, init_im)
    return out, fr, fi


D_Q = N_HEADS * HEAD_DIM
KV_DIM = N_KV_HEADS * HEAD_DIM
N_KINDS = 6
N_GATES = 3 * N_HEADS
HALF = HEAD_DIM // 2


def _rope_tables(pos):
    inv = jnp.power(ROPE_THETA, -jnp.arange(HALF, dtype=F32) / HALF)
    ang = inv[:, None] * pos.astype(F32)[None, :]
    return jnp.cos(ang), jnp.sin(ang)


def _nsa_proj_kernel(x_ref, g_ref, wt_ref, qg_ref, kg_ref, cos_ref, sin_ref, q_ref, gt_ref, *rest, tm, paged):
    if paged:
        kind_refs, (ksa_ref, vsa_ref, kw_ref, vwa_ref, kcp_ref, vcp_ref, pt_s) = rest[:N_KINDS], rest[N_KINDS:]
        n_sel = ksa_ref.shape[1] - HEAD_DIM
        key = pl.program_id(1) * tm + lax.broadcasted_iota(jnp.int32, (n_sel, tm), 1)
        expand = jnp.where(key // L_SEL == lax.broadcasted_iota(jnp.int32, (n_sel, tm), 0), 1.0, 0.0).astype(BF16)
        ones_row = jnp.where(lax.broadcasted_iota(jnp.int32, (BF16_SUBLANES, tm), 0) == 0, 1.0, 0.0).astype(BF16)
    else:
        kv_ref, pt_s = rest
    xn = _rms(x_ref[...], g_ref[...]).astype(BF16)
    pt_s[...] = lax.dot_general(wt_ref[...], xn, (((1,), (1,)), ((), ())), preferred_element_type=F32)
    cos, sin = cos_ref[...], sin_ref[...]

    def norm_rope(blk, gain):
        y = blk * lax.rsqrt(jnp.mean(blk * blk, axis=0, keepdims=True) + NORM_EPS) * gain
        x1, x2 = y[:HALF], y[HALF:]
        return x1 * cos - x2 * sin, x2 * cos + x1 * sin

    scale = HEAD_DIM ** -0.5
    for h in range(N_HEADS):
        r0 = h * HEAD_DIM
        r1, r2 = norm_rope(pt_s[r0:r0 + HEAD_DIM, :], qg_ref[...])
        q_ref[r0:r0 + HALF, :] = (r1 * scale).astype(BF16)
        q_ref[r0 + HALF:r0 + HEAD_DIM, :] = (r2 * scale).astype(BF16)
    for kind in range(N_KINDS):
        for gi in range(N_KV_HEADS):
            r0 = gi * HEAD_DIM
            blk = pt_s[D_Q + kind * KV_DIM + r0:D_Q + kind * KV_DIM + r0 + HEAD_DIM, :]
            if kind % 2 == 0:
                r1, r2 = norm_rope(blk, kg_ref[kind // 2])
                blk = jnp.concatenate([r1, r2], axis=0)
            if not paged:
                kv_ref[kind, r0:r0 + HEAD_DIM, :] = blk
                continue
            kind_refs[kind][r0:r0 + HEAD_DIM, :] = blk
            if kind < 2:
                dst = kcp_ref if kind == 0 else vcp_ref
                for p in range(tm // PAGE_SIZE):
                    dst[p, gi] = blk[:, p * PAGE_SIZE:(p + 1) * PAGE_SIZE]
            elif kind == 2:
                ksa_ref[gi, 0:n_sel, :] = expand
                ksa_ref[gi, n_sel:n_sel + HEAD_DIM, :] = blk.astype(BF16)
            elif kind == 4:
                kw_ref[gi] = blk.astype(BF16)
            else:
                dst = vsa_ref if kind == 3 else vwa_ref
                dst[gi, 0:HEAD_DIM, :] = blk.astype(BF16)
                dst[gi, HEAD_DIM:HEAD_DIM + BF16_SUBLANES, :] = ones_row
    g0 = D_Q + N_KINDS * KV_DIM
    gt_ref[...] = jax.nn.sigmoid(pt_s[g0:g0 + N_GATES, :])


def _nsa_project(x, pos, g, w_in, q_gain, k_gain, *, tm, paged):
    S, R, D = x.shape
    NP = w_in.shape[1]
    G = N_KV_HEADS
    assert R % tm == 0 and (not paged or tm % PAGE_SIZE == 0)
    nr = R // tm
    wt = w_in.T.astype(BF16)
    cos, sin = _rope_tables(pos)
    qg = jnp.broadcast_to(q_gain.astype(F32)[:, None], (HEAD_DIM, tm))
    kg = jnp.broadcast_to(k_gain.astype(F32)[:, :, None], (3, HEAD_DIM, tm))
    out_specs = [
        pl.BlockSpec((None, D_Q, tm), lambda s, i: (s, 0, i)),
        pl.BlockSpec((None, N_GATES, tm), lambda s, i: (s, 0, i)),
    ]
    out_shape = [jax.ShapeDtypeStruct((S, D_Q, R), BF16), jax.ShapeDtypeStruct((S, N_GATES, R), F32)]
    if paged:
        ppt = tm // PAGE_SIZE
        va = HEAD_DIM + BF16_SUBLANES
        grouped = lambda rows: pl.BlockSpec((None, G, rows, tm), lambda s, i: (s, 0, 0, i))
        out_specs += [pl.BlockSpec((None, KV_DIM, tm), lambda s, i: (s, 0, i))] * N_KINDS
        out_shape += [jax.ShapeDtypeStruct((S, KV_DIM, R), F32)] * N_KINDS
        for rows in (R // L_SEL + HEAD_DIM, va, HEAD_DIM, va):
            out_specs.append(grouped(rows))
            out_shape.append(jax.ShapeDtypeStruct((S, G, rows, R), BF16))
        for _ in range(2):
            out_specs.append(pl.BlockSpec((ppt, G, HEAD_DIM, PAGE_SIZE), lambda s, i: (s * nr + i, 0, 0, 0)))
            out_shape.append(jax.ShapeDtypeStruct((S * R // PAGE_SIZE, G, HEAD_DIM, PAGE_SIZE), F32))
    else:
        out_specs.append(pl.BlockSpec((None, N_KINDS, KV_DIM, tm), lambda s, i: (s, 0, 0, i)))
        out_shape.append(jax.ShapeDtypeStruct((S, N_KINDS, KV_DIM, R), F32))
    return pl.pallas_call(
        functools.partial(_nsa_proj_kernel, tm=tm, paged=paged),
        grid=(S, nr),
        in_specs=[
            pl.BlockSpec((None, tm, D), lambda s, i: (s, i, 0)),
            pl.BlockSpec((1, D), lambda s, i: (0, 0)),
            pl.BlockSpec((NP, D), lambda s, i: (0, 0)),
            pl.BlockSpec((HEAD_DIM, tm), lambda s, i: (0, 0)),
            pl.BlockSpec((3, HEAD_DIM, tm), lambda s, i: (0, 0, 0)),
            pl.BlockSpec((HALF, tm), lambda s, i: (0, i)),
            pl.BlockSpec((HALF, tm), lambda s, i: (0, i)),
        ],
        out_specs=out_specs,
        out_shape=out_shape,
        scratch_shapes=[pltpu.VMEM((NP, tm), F32)],
        compiler_params=_cparams(("arbitrary", "arbitrary")),
        name="nsa_project",
    )(x, g.reshape(1, D), wt, qg, kg, cos, sin)


PAGE_ROWS = N_KV_HEADS * HEAD_DIM
CMP_PER_PAGE = PAGE_SIZE // L_CMP


def _page_copies(pages_hbm, buf, sem, page, slot, p, by_feature):
    if not by_feature:
        return [pltpu.make_async_copy(pages_hbm.at[page], buf.at[slot, pl.ds(p * PAGE_ROWS, PAGE_ROWS), :],
                                      sem.at[slot])]
    return [pltpu.make_async_copy(pages_hbm.at[page, pl.ds(g * HEAD_DIM, HEAD_DIM), :],
                                  buf.at[slot, :, p * N_KV_HEADS + g, :], sem.at[slot])
            for g in range(N_KV_HEADS)]


def _fetch_chunk(pt_ref, pages_hbm, buf, sem, n, slot, *, nchunks, npc, by_feature):
    s = n // nchunks
    c = n % nchunks

    def start(p, carry):
        for cp in _page_copies(pages_hbm, buf, sem, pt_ref[s, c * npc + p], slot, p, by_feature):
            cp.start()
        return carry

    lax.fori_loop(0, npc, start, 0)


def _wait_chunk(pages_hbm, buf, sem, slot, *, npc, by_feature):
    def wait(p, carry):
        for cp in _page_copies(pages_hbm, buf, sem, 0, slot, p, by_feature):
            cp.wait()
        return carry

    lax.fori_loop(0, npc, wait, 0)


def _stream_pages(pt_ref, pages_hbm, buf, sem, *, nchunks, npc, total, by_feature=False):
    n = pl.program_id(0) * nchunks + pl.program_id(1)
    slot = n % 2
    kw = dict(nchunks=nchunks, npc=npc, by_feature=by_feature)

    @pl.when(n == 0)
    def _():
        _fetch_chunk(pt_ref, pages_hbm, buf, sem, n, slot, **kw)

    @pl.when(n + 1 < total)
    def _():
        _fetch_chunk(pt_ref, pages_hbm, buf, sem, n + 1, 1 - slot, **kw)

    _wait_chunk(pages_hbm, buf, sem, slot, npc=npc, by_feature=by_feature)
    return slot


def _compress_tables(w1, b1, w2):
    eye = jnp.eye(CMP_PER_PAGE, dtype=F32)
    m = jnp.einsum('lde,ck->dclke', w1.astype(F32), eye)
    m = m.reshape(HEAD_DIM * PAGE_SIZE, CMP_PER_PAGE * HEAD_DIM).astype(BF16)
    b1t = jnp.tile(b1.astype(F32), CMP_PER_PAGE).reshape(1, CMP_PER_PAGE * HEAD_DIM)
    w2bd = jnp.einsum('ef,ck->cekf', w2.astype(F32), eye)
    w2bd = w2bd.reshape(CMP_PER_PAGE * HEAD_DIM, CMP_PER_PAGE * HEAD_DIM).astype(BF16)
    return m, b1t, w2bd


def _compress_kernel(pt_ref, pages_hbm, m_ref, b1_ref, w2_ref, o_ref, buf, sem, lhs_s, *, nchunks, npc, total):
    slot = _stream_pages(pt_ref, pages_hbm, buf, sem, nchunks=nchunks, npc=npc, total=total, by_feature=True)
    for d in range(HEAD_DIM):
        lhs_s[:, d * PAGE_SIZE:(d + 1) * PAGE_SIZE] = buf[slot, d].astype(BF16)
    acc = jnp.dot(lhs_s[...], m_ref[...], preferred_element_type=F32)
    h = jax.nn.gelu(acc + b1_ref[...])
    o_ref[...] = jnp.dot(h.astype(BF16), w2_ref[...], preferred_element_type=F32)


def _compress(pages, page_table, tables, *, npc):
    S, NPG = page_table.shape
    assert NPG % npc == 0
    nchunks = NPG // npc
    m, b1t, w2bd = tables
    nrow = npc * N_KV_HEADS
    nce = CMP_PER_PAGE * HEAD_DIM
    pages2 = pages.reshape(pages.shape[0], PAGE_ROWS, PAGE_SIZE)
    kern = functools.partial(_compress_kernel, nchunks=nchunks, npc=npc, total=S * nchunks)
    out = pl.pallas_call(
        kern,
        grid_spec=pltpu.PrefetchScalarGridSpec(
            num_scalar_prefetch=1,
            grid=(S, nchunks),
            in_specs=[
                pl.BlockSpec(memory_space=pl.ANY),
                pl.BlockSpec(m.shape, lambda s, c, pt: (0, 0)),
                pl.BlockSpec(b1t.shape, lambda s, c, pt: (0, 0)),
                pl.BlockSpec(w2bd.shape, lambda s, c, pt: (0, 0)),
            ],
            out_specs=pl.BlockSpec((None, nrow, nce), lambda s, c, pt: (s, c, 0)),
            scratch_shapes=[pltpu.VMEM((2, HEAD_DIM, nrow, PAGE_SIZE), F32), pltpu.SemaphoreType.DMA((2,)),
                            pltpu.VMEM((nrow, HEAD_DIM * PAGE_SIZE), BF16)],
        ),
        out_shape=jax.ShapeDtypeStruct((S, NPG * N_KV_HEADS, nce), F32),
        compiler_params=_cparams(("arbitrary", "arbitrary")),
        name="nsa_compress",
    )(page_table, pages2, m, b1t, w2bd)
    out = out.reshape(S, NPG, N_KV_HEADS, CMP_PER_PAGE, HEAD_DIM).transpose(0, 1, 3, 2, 4)
    return out.reshape(S, NPG * CMP_PER_PAGE, N_KV_HEADS, HEAD_DIM)


HPG = N_HEADS // N_KV_HEADS
CMP_PER_SEL = L_SEL // L_CMP
NT_DIMS = (((1,), (1,)), ((), ()))


def _masked_softmax(s, mask):
    s = jnp.where(mask, s, NEG_INF)
    e = jnp.exp(s - jnp.max(s, axis=-1, keepdims=True))
    return jnp.where(mask, e / jnp.sum(e, axis=-1, keepdims=True), 0.0)


def _select_blocks(imp, pos_q, n_sel):
    nq = imp.shape[0]
    jidx = lax.broadcasted_iota(jnp.int32, (nq, n_sel), 1)
    qblk = pos_q // L_SEL
    valid = jidx * L_SEL <= pos_q
    forced = (jidx == 0) | (jidx == qblk) | (jidx == qblk - 1)
    score = jnp.where(valid & forced, FORCE_SCORE, jnp.where(valid, imp, NEG_INF))
    sct = score.T
    jrow = lax.broadcasted_iota(jnp.int32, (n_sel, nq), 0)
    sel = jnp.zeros((n_sel, nq), F32)
    for _ in range(min(TOP_N, n_sel)):
        top = jnp.max(sct, axis=0, keepdims=True)
        first = jnp.min(jnp.where(sct == top, jrow, n_sel), axis=0, keepdims=True)
        hit = jrow == first
        sct = jnp.where(hit, -jnp.inf, sct)
        sel = jnp.where(hit, 1.0, sel)
    return sel


def _cmp_block_of_column(ncol):
    c = lax.broadcasted_iota(jnp.int32, (1, ncol), 1)
    half = ncol // CMP_PER_SEL
    return jnp.where(c < half, CMP_PER_SEL * c, CMP_PER_SEL * (c - half) + 1)


Q_BLOCK = 128
GROUPS_PER_STEP = 4
BF16_SUBLANES = 2 * SUBLANES


M_FLOOR = 0.1 * NEG_INF


def _bias_softmax(s, bias):
    s = s + bias
    e = jnp.exp(s - jnp.maximum(jnp.max(s, axis=-1, keepdims=True), M_FLOOR))
    l = jnp.sum(e, axis=-1, keepdims=True)
    return e, 1.0 / jnp.where(l > 0.0, l, 1.0)


def _attn_prompt_kernel(q_ref, kc_ref, vc_ref, ksa_ref, vsa_ref, kw_ref, vwa_ref, gt_ref, o_ref, *, tk, T, gb):
    qi = pl.program_id(2)
    nq = Q_BLOCK
    nr = HPG * nq
    nc = kc_ref.shape[2]
    n_sel = nc // CMP_PER_SEL
    va = vsa_ref.shape[1]
    pos_q = qi * nq + lax.broadcasted_iota(jnp.int32, (nq, 1), 0)
    cbias = jnp.where((_cmp_block_of_column(nc) * L_CMP + (L_CMP - 1)) <= pos_q, 0.0, NEG_INF)

    def normalise(r):
        l = r[..., HEAD_DIM:HEAD_DIM + 1]
        return r[..., :HEAD_DIM] * (1.0 / jnp.where(l > 0.0, l, 1.0))

    qs, q_augs, o_cs = [], [], []
    for gl in range(gb):
        qrows = q_ref[gl].astype(F32).T
        q = jnp.concatenate([qrows[:, h * HEAD_DIM:(h + 1) * HEAD_DIM] for h in range(HPG)], axis=0)
        q = q.astype(BF16)
        s_c = jnp.dot(q, kc_ref[gl], preferred_element_type=F32).reshape(HPG, nq, nc)
        e_c, inv_c = _bias_softmax(s_c, cbias[None])
        p_c = e_c * inv_c
        o_cs.append(jnp.dot(p_c.reshape(nr, nc).astype(BF16), vc_ref[gl],
                            preferred_element_type=F32).reshape(HPG, nq, HEAD_DIM))
        imp = p_c[0]
        for h in range(1, HPG):
            imp = imp + p_c[h]
        imp = imp[:, :n_sel] + imp[:, n_sel:]
        sel = _select_blocks(imp, pos_q, n_sel).T
        unsel = jnp.where(sel > 0.5, 0.0, NEG_INF).astype(BF16)
        qs.append(q)
        q_augs.append(jnp.concatenate([jnp.concatenate([unsel] * HPG, axis=0), q], axis=1))

    def sel_tile(gl, k0, carry, causal):
        m, acc = carry
        s = jnp.dot(q_augs[gl], ksa_ref[gl, :, pl.ds(k0, tk)], preferred_element_type=F32)
        s = s.reshape(HPG, nq, tk)
        if causal:
            kpos = k0 + lax.broadcasted_iota(jnp.int32, (nq, tk), 1)
            s = s + jnp.where(kpos <= pos_q, 0.0, NEG_INF)[None]
        m_new = jnp.maximum(m, jnp.max(s, axis=-1, keepdims=True))
        p = jnp.exp(s - m_new).astype(BF16)
        pv = lax.dot_general(p.reshape(nr, tk), vsa_ref[gl, :, pl.ds(k0, tk)], NT_DIMS,
                             preferred_element_type=F32)
        return m_new, jnp.exp(m - m_new) * acc + pv.reshape(HPG, nq, va)

    def sel_tiles(k0, carries, causal):
        return tuple(sel_tile(gl, k0, carries[gl], causal) for gl in range(gb))

    last = (qi * nq + nq + tk - 1) // tk - 1
    init = (jnp.full((HPG, nq, 1), M_FLOOR, F32), jnp.zeros((HPG, nq, va), F32))
    carries = lax.fori_loop(0, last, lambda kt, c: sel_tiles(pl.multiple_of(kt * tk, tk), c, False),
                            (init,) * gb)
    carries = sel_tiles(pl.multiple_of(last * tk, tk), carries, True)

    wl = min(WINDOW + nq, T)
    w0 = pl.multiple_of(jnp.maximum(qi * nq + nq - wl, 0), LANES)
    dpos = pos_q - (w0 + lax.broadcasted_iota(jnp.int32, (nq, wl), 1))
    wbias = jnp.where((dpos >= 0) & (dpos < WINDOW), 0.0, NEG_INF)[None]
    for gl in range(gb):
        o_s = normalise(carries[gl][1])
        s_w = jnp.dot(qs[gl], kw_ref[gl, :, pl.ds(w0, wl)], preferred_element_type=F32).reshape(HPG, nq, wl)
        s_w = s_w + wbias
        e_w = jnp.exp(s_w - jnp.maximum(jnp.max(s_w, axis=-1, keepdims=True), M_FLOOR)).astype(BF16)
        o_w = normalise(lax.dot_general(e_w.reshape(nr, wl), vwa_ref[gl, :, pl.ds(w0, wl)], NT_DIMS,
                                        preferred_element_type=F32).reshape(HPG, nq, va))
        gt = gt_ref[gl]
        o_c = o_cs[gl]
        outs = []
        for h in range(HPG):
            outs.append(gt[:, h:h + 1] * o_c[h] + gt[:, HPG + h:HPG + h + 1] * o_s[h]
                        + gt[:, 2 * HPG + h:2 * HPG + h + 1] * o_w[h])
        o_ref[:, gl * HPG * HEAD_DIM:(gl + 1) * HPG * HEAD_DIM] = jnp.concatenate(outs, axis=1).astype(BF16)


def _block_expansion(n_sel, n_keys):
    return (jnp.arange(n_keys)[None, :] // L_SEL == jnp.arange(n_sel)[:, None]).astype(BF16)


def _cmp_layouts(kc, vc):
    order = jnp.concatenate([jnp.arange(0, kc.shape[1], CMP_PER_SEL), jnp.arange(1, kc.shape[1], CMP_PER_SEL)])
    kct = kc[:, order].transpose(0, 2, 3, 1).astype(BF16)
    vcr = vc[:, order].transpose(0, 2, 1, 3).astype(BF16)
    return kct, vcr


def _attend_prompt(qT, ks_aug, vs_aug, kw, vw_aug, gT, kc, vc, *, tk):
    B, _, T = qT.shape
    G = N_KV_HEADS
    nq = Q_BLOCK
    assert T % tk == 0 and T % nq == 0 and tk % L_SEL == 0
    nc = T // L_CMP
    n_sel = T // L_SEL
    kct, vcr = _cmp_layouts(kc, vc)
    q4 = qT.reshape(B, G, HPG * HEAD_DIM, T)
    gates = gT.reshape(B, 3, G, HPG, T).transpose(0, 2, 4, 1, 3).reshape(B, G, T, 3 * HPG)
    va = HEAD_DIM + BF16_SUBLANES
    gb = GROUPS_PER_STEP
    assert G % gb == 0
    once = pl.Buffered(1)
    per_group = lambda rows: pl.BlockSpec((None, gb, rows, T), lambda b, g, i: (b, g, 0, 0), pipeline_mode=once)
    return pl.pallas_call(
        functools.partial(_attn_prompt_kernel, tk=tk, T=T, gb=gb),
        grid=(B, G // gb, T // nq),
        in_specs=[
            pl.BlockSpec((None, gb, HPG * HEAD_DIM, nq), lambda b, g, i: (b, g, 0, i)),
            pl.BlockSpec((None, gb, HEAD_DIM, nc), lambda b, g, i: (b, g, 0, 0)),
            pl.BlockSpec((None, gb, nc, HEAD_DIM), lambda b, g, i: (b, g, 0, 0)),
            per_group(n_sel + HEAD_DIM), per_group(va), per_group(HEAD_DIM), per_group(va),
            pl.BlockSpec((None, gb, nq, 3 * HPG), lambda b, g, i: (b, g, i, 0)),
        ],
        out_specs=pl.BlockSpec((None, nq, gb * HPG * HEAD_DIM), lambda b, g, i: (b, i, g)),
        out_shape=jax.ShapeDtypeStruct((B, T, D_Q), BF16),
        compiler_params=_cparams(("arbitrary", "arbitrary", "arbitrary")),
        name="nsa_attend_prompt",
    )(q4, kct, vcr, ks_aug, vs_aug, kw, vw_aug, gates)


PAGE_LOOP_UNROLL = 8


def _attn_sample_kernel(pt_ref, q_ref, kc_ref, vc_ref, kwc_ref, vwc_ref, kwn_ref, vwn_ref, ksn_ref, vsn_ref,
                        gt_ref, e_ref, dm_ref, ks_hbm, vs_hbm, o_ref,
                        kbuf, vbuf, ksem, vsem, s_s, sel_s, m_s, l_s, acc_s, oc_s, ow_s,
                        *, nchunks, npc, total, past, tq):
    c = pl.program_id(1)
    kslot = _stream_pages(pt_ref, ks_hbm, kbuf, ksem, nchunks=nchunks, npc=npc, total=total)
    vslot = _stream_pages(pt_ref, vs_hbm, vbuf, vsem, nchunks=nchunks, npc=npc, total=total)
    q = q_ref[...]
    nr = q.shape[0]
    ngt = N_KV_HEADS * tq
    pos_q = past + lax.broadcasted_iota(jnp.int32, (nr, 1), 0) % tq
    n_sel = sel_s.shape[1]
    sel_past = past // L_SEL

    def fold(o_full):
        o = o_full * dm_ref[...]
        out = o[:, 0:HEAD_DIM]
        for gi in range(1, N_KV_HEADS):
            out = out + o[:, gi * HEAD_DIM:(gi + 1) * HEAD_DIM]
        return out

    def nt(p, vt):
        return lax.dot_general(p.astype(BF16), vt.astype(BF16), NT_DIMS, preferred_element_type=F32)

    @pl.when(c == 0)
    def _():
        nc = kc_ref.shape[1]
        s_c = jnp.dot(q, kc_ref[...], preferred_element_type=F32)
        cmask = (_cmp_block_of_column(nc) * L_CMP + (L_CMP - 1)) <= pos_q
        p_c = _masked_softmax(s_c, cmask)
        oc_s[...] = fold(nt(p_c, vc_ref[...]))
        imp = p_c[0:ngt]
        for h in range(1, HPG):
            imp = imp + p_c[h * ngt:(h + 1) * ngt]
        imp = imp[:, :n_sel] + imp[:, n_sel:]
        pad = LANES - ngt
        imp = jnp.concatenate([imp, jnp.zeros((pad, n_sel), F32)], axis=0)
        pos_pad = past + lax.broadcasted_iota(jnp.int32, (LANES, 1), 0) % tq
        sel = _select_blocks(imp, pos_pad, n_sel).T[0:ngt]
        sel_s[...] = jnp.concatenate([sel] * HPG, axis=0)

        nwc = kwc_ref.shape[1]
        nwn = kwn_ref.shape[1]
        s_w = jnp.concatenate([jnp.dot(q, kwc_ref[...].astype(BF16), preferred_element_type=F32),
                               jnp.dot(q, kwn_ref[...].astype(BF16), preferred_element_type=F32)], axis=1)
        lane = lax.broadcasted_iota(jnp.int32, (nr, nwc + nwn), 1)
        pos_w = jnp.where(lane < nwc, past - nwc + lane, past + lane - nwc)
        dpos = pos_q - pos_w
        p_w = _masked_softmax(s_w, (dpos >= 0) & (dpos < WINDOW) & (pos_w >= 0))
        ow_s[...] = fold(nt(p_w[:, :nwc], vwc_ref[...]) + nt(p_w[:, nwc:], vwn_ref[...]))

        nsn = ksn_ref.shape[1]
        s_n = jnp.dot(q, ksn_ref[...].astype(BF16), preferred_element_type=F32)
        kpos = past + lax.broadcasted_iota(jnp.int32, (nr, nsn), 1)
        msk = (sel_s[:, sel_past:sel_past + 1] > 0.5) & (kpos <= pos_q)
        s_n = jnp.where(msk, s_n, NEG_INF)
        m0 = jnp.max(s_n, axis=-1, keepdims=True)
        p_n = jnp.where(msk, jnp.exp(s_n - m0), 0.0)
        m_s[...] = m0
        l_s[...] = jnp.sum(p_n, axis=-1, keepdims=True)
        acc_s[...] = nt(p_n, vsn_ref[...])

    def qk(p, carry):
        r0 = pl.multiple_of(p * PAGE_ROWS, PAGE_ROWS)
        c0 = pl.multiple_of(p * PAGE_SIZE, PAGE_SIZE)
        s_s[:, pl.ds(c0, PAGE_SIZE)] = jnp.dot(q, kbuf[kslot, pl.ds(r0, PAGE_ROWS), :].astype(BF16),
                                                preferred_element_type=F32)
        return carry

    lax.fori_loop(0, npc, qk, 0, unroll=PAGE_LOOP_UNROLL)
    nk = npc * PAGE_SIZE
    sel_c = sel_s[:, pl.ds(pl.multiple_of(c * (nk // L_SEL), LANES), nk // L_SEL)]
    chosen = jnp.dot(sel_c.astype(BF16), e_ref[...], preferred_element_type=F32)
    kpos = c * nk + lax.broadcasted_iota(jnp.int32, (nr, nk), 1)
    msk = (chosen > 0.5) & (kpos <= pos_q)
    s = jnp.where(msk, s_s[...], NEG_INF)
    m_old = m_s[...]
    m_new = jnp.maximum(m_old, jnp.max(s, axis=-1, keepdims=True))
    alpha = jnp.exp(m_old - m_new)
    p = jnp.where(msk, jnp.exp(s - m_new), 0.0)
    l_s[...] = alpha * l_s[...] + jnp.sum(p, axis=-1, keepdims=True)
    m_s[...] = m_new
    s_s[...] = p

    def pv(p_, acc):
        r0 = pl.multiple_of(p_ * PAGE_ROWS, PAGE_ROWS)
        c0 = pl.multiple_of(p_ * PAGE_SIZE, PAGE_SIZE)
        return acc + nt(s_s[:, pl.ds(c0, PAGE_SIZE)], vbuf[vslot, pl.ds(r0, PAGE_ROWS), :])

    acc = lax.fori_loop(0, npc, pv, jnp.zeros((nr, PAGE_ROWS), F32), unroll=PAGE_LOOP_UNROLL)
    acc_s[...] = alpha * acc_s[...] + acc

    @pl.when(c == nchunks - 1)
    def _():
        l = l_s[...]
        o_sel = fold(acc_s[...]) / jnp.where(l > 0.0, l, 1.0)
        gt = gt_ref[...]
        o_ref[...] = gt[:, 0:1] * oc_s[...] + gt[:, 1:2] * o_sel + gt[:, 2:3] * ow_s[...]


def _attend_sample(qT, kvT, gT, kc_all, vc_all, win_k, win_v, pool_ks, pool_vs, page_table, *, npc):
    Bs, NPG = page_table.shape
    tq = qT.shape[1] // Bs
    G = N_KV_HEADS
    past = NPG * PAGE_SIZE
    nchunks = NPG // npc
    nk = npc * PAGE_SIZE
    assert NPG % npc == 0 and nk // L_SEL == LANES
    nr = HPG * G * tq
    eye = jnp.eye(G, dtype=qT.dtype)
    q5 = qT.reshape(G, HPG, HEAD_DIM, tq, Bs).transpose(4, 1, 0, 3, 2)
    q_bd = jnp.einsum('bhgtd,gk->bhgtkd', q5, eye).reshape(Bs, nr, G * HEAD_DIM)
    dm = jnp.broadcast_to(jnp.eye(G, dtype=F32)[None, :, None, :, None],
                          (HPG, G, tq, G, HEAD_DIM)).reshape(nr, G * HEAD_DIM)
    gates = gT.reshape(3, G, HPG, tq, Bs).transpose(4, 2, 1, 3, 0).reshape(Bs, nr, 3)

    def new_rows(kind):
        r = kvT[kind].reshape(G * HEAD_DIM, tq, Bs).transpose(2, 0, 1)
        return jnp.pad(r, ((0, 0), (0, 0), (0, LANES - tq)))

    nc = kc_all.shape[1]
    half = nc // CMP_PER_SEL
    n_sel = -(-(half) // LANES) * LANES
    assert past // L_SEL < n_sel

    def cmp_fm(x):
        padw = ((0, 0), (0, 0), (0, n_sel - half))
        parts = [jnp.pad(x[:, par::CMP_PER_SEL].transpose(0, 2, 3, 1).reshape(Bs, G * HEAD_DIM, half), padw)
                 for par in range(CMP_PER_SEL)]
        return jnp.concatenate(parts, axis=2).astype(BF16)

    n_win = win_k.shape[-1]
    wk = win_k.reshape(Bs, G * HEAD_DIM, n_win)
    wv = win_v.reshape(Bs, G * HEAD_DIM, n_win)
    e = _block_expansion(nk // L_SEL, nk)
    ks2 = pool_ks.reshape(pool_ks.shape[0], PAGE_ROWS, PAGE_SIZE)
    vs2 = pool_vs.reshape(pool_vs.shape[0], PAGE_ROWS, PAGE_SIZE)
    per_b = lambda shape: pl.BlockSpec((None,) + shape, lambda b, c, pt: (b, 0, 0))
    kern = functools.partial(_attn_sample_kernel, nchunks=nchunks, npc=npc, total=Bs * nchunks, past=past, tq=tq)
    o = pl.pallas_call(
        kern,
        grid_spec=pltpu.PrefetchScalarGridSpec(
            num_scalar_prefetch=1,
            grid=(Bs, nchunks),
            in_specs=[
                per_b((nr, G * HEAD_DIM)),
                per_b((G * HEAD_DIM, 2 * n_sel)), per_b((G * HEAD_DIM, 2 * n_sel)),
                per_b((G * HEAD_DIM, n_win)), per_b((G * HEAD_DIM, n_win)),
                per_b((G * HEAD_DIM, LANES)), per_b((G * HEAD_DIM, LANES)),
                per_b((G * HEAD_DIM, LANES)), per_b((G * HEAD_DIM, LANES)),
                per_b((nr, 3)),
                pl.BlockSpec(e.shape, lambda b, c, pt: (0, 0)),
                pl.BlockSpec(dm.shape, lambda b, c, pt: (0, 0)),
                pl.BlockSpec(memory_space=pl.ANY),
                pl.BlockSpec(memory_space=pl.ANY),
            ],
            out_specs=pl.BlockSpec((None, nr, HEAD_DIM), lambda b, c, pt: (b, 0, 0)),
            scratch_shapes=[
                pltpu.VMEM((2, npc * PAGE_ROWS, PAGE_SIZE), F32), pltpu.VMEM((2, npc * PAGE_ROWS, PAGE_SIZE), F32),
                pltpu.SemaphoreType.DMA((2,)), pltpu.SemaphoreType.DMA((2,)),
                pltpu.VMEM((nr, nk), F32), pltpu.VMEM((nr, n_sel), F32),
                pltpu.VMEM((nr, 1), F32), pltpu.VMEM((nr, 1), F32), pltpu.VMEM((nr, G * HEAD_DIM), F32),
                pltpu.VMEM((nr, HEAD_DIM), F32), pltpu.VMEM((nr, HEAD_DIM), F32),
            ],
        ),
        out_shape=jax.ShapeDtypeStruct((Bs, nr, HEAD_DIM), F32),
        compiler_params=_cparams(("arbitrary", "arbitrary")),
        name="nsa_attend_sample",
    )(page_table, q_bd, cmp_fm(kc_all), cmp_fm(vc_all), wk, wv, new_rows(4), new_rows(5), new_rows(2), new_rows(3),
      gates, e, dm, ks2, vs2)
    o = o.reshape(Bs, HPG, G, tq, HEAD_DIM).transpose(3, 0, 2, 1, 4)
    return o.reshape(tq * Bs, D_Q).astype(BF16)


def _out_proj_kernel(a_ref, w_ref, x_ref, o_ref):
    o_ref[...] = x_ref[...] + jnp.dot(a_ref[...], w_ref[...], preferred_element_type=F32)


def _out_proj(a, w, x, *, tm):
    R, Kd = a.shape
    N = w.shape[1]
    assert R % tm == 0
    return pl.pallas_call(
        _out_proj_kernel,
        grid=(R // tm,),
        in_specs=[pl.BlockSpec((tm, Kd), lambda i: (i, 0)), pl.BlockSpec((Kd, N), lambda i: (0, 0)),
                  pl.BlockSpec((tm, N), lambda i: (i, 0))],
        out_specs=pl.BlockSpec((tm, N), lambda i: (i, 0)),
        out_shape=jax.ShapeDtypeStruct((R, N), F32),
        compiler_params=_cparams(("arbitrary",)),
        name="out_proj",
    )(a, w.astype(BF16), x)


PROMPT_ROW_TILE = 512
S5_CHUNK = 256
SEL_KEY_TILE = 1024
SAMPLE_PAGES_PER_CHUNK = 64


def _feature_major(cache):
    nd = cache.ndim
    return jnp.moveaxis(cache, nd - 3, nd - 1)


def _nsa_prompt_layer(x, g, w_in, q_gain, k_gain, ck, cv, w_out):
    B, T, D = x.shape
    G = N_KV_HEADS
    outs = _nsa_project(x, jnp.arange(T), g, w_in, q_gain, k_gain, tm=PROMPT_ROW_TILE, paged=True)
    qT, gT = outs[:2]
    kinds = outs[2:2 + N_KINDS]
    ks_aug, vs_aug, kw, vw_aug, kcp, vcp = outs[2 + N_KINDS:]
    npg = T // PAGE_SIZE
    ptab = jnp.arange(B * npg, dtype=jnp.int32).reshape(B, npg)
    kc = _compress(kcp, ptab, _compress_tables(*ck), npc=npg)
    vc = _compress(vcp, ptab, _compress_tables(*cv), npc=npg)
    o = _attend_prompt(qT, ks_aug, vs_aug, kw, vw_aug, gT, kc, vc, tk=SEL_KEY_TILE)
    y = _out_proj(o.reshape(B * T, D_Q), w_out, x.reshape(B * T, D), tm=PROMPT_ROW_TILE).reshape(B, T, D)
    rows = [k.reshape(B, G, HEAD_DIM, T).transpose(0, 3, 1, 2) for k in kinds]
    n_keep = min(WINDOW, T)
    rows[4] = rows[4][:, T - n_keep:]
    rows[5] = rows[5][:, T - n_keep:]
    return y, rows


def _nsa_sample_layer(x_tm, Bs, pools, win_k, win_v, page_table, g, w_in, q_gain, k_gain, ck, cv, w_out):
    _, R, D = x_tm.shape
    G = N_KV_HEADS
    tq = R // Bs
    npg = page_table.shape[1]
    past = npg * PAGE_SIZE
    pos = past + jnp.arange(R) // Bs
    qT, gT, kvT = _nsa_project(x_tm, pos, g, w_in, q_gain, k_gain, tm=R, paged=False)
    qT, kvT, gT = qT[0], kvT[0], gT[0]
    new_pad = -(-tq // L_SEL) * L_SEL
    assert new_pad <= PAGE_SIZE
    new_tab = jnp.arange(Bs, dtype=jnp.int32).reshape(1, Bs)

    def cmp_all(pool, kind, tabs):
        past_blocks = _compress(_feature_major(pool), page_table, tabs, npc=SAMPLE_PAGES_PER_CHUNK)
        page = kvT[kind].reshape(G, HEAD_DIM, tq, Bs).transpose(3, 0, 1, 2)
        page = jnp.pad(page, ((0, 0), (0, 0), (0, 0), (0, PAGE_SIZE - tq)))
        new_blocks = _compress(page, new_tab, tabs, npc=Bs).reshape(Bs, CMP_PER_PAGE, G, HEAD_DIM)
        return jnp.concatenate([past_blocks, new_blocks[:, :new_pad // L_CMP]], axis=1)

    kc_all = cmp_all(pools[0], 0, _compress_tables(*ck))
    vc_all = cmp_all(pools[1], 1, _compress_tables(*cv))
    o = _attend_sample(qT, kvT, gT, kc_all, vc_all, _feature_major(win_k), _feature_major(win_v),
                       _feature_major(pools[2]), _feature_major(pools[3]), page_table,
                       npc=SAMPLE_PAGES_PER_CHUNK)
    y = _out_proj(o, w_out, x_tm[0], tm=R)[None]
    rows = [kvT[k].reshape(G, HEAD_DIM, tq, Bs).transpose(3, 2, 0, 1) for k in range(N_KINDS)]
    return y, rows


def kernel(x_prompt, x_sample, cache_k_cmp, cache_v_cmp, cache_k_slc, cache_v_slc, cache_k_win, cache_v_win,
           state_ssm_re, state_ssm_im, state_conv, page_table, norm_mix, norm_ffn,
           ssm_w_in, ssm_a_re, ssm_a_im, ssm_log_dt, ssm_b_re, ssm_b_im, ssm_c_re, ssm_c_im,
           ssm_d, ssm_w_gate, ssm_w_out,
           nsa_w_in, nsa_q_gain, nsa_k_gain, nsa_ck_w1, nsa_ck_b1, nsa_ck_w2,
           nsa_cv_w1, nsa_cv_b1, nsa_cv_w2, nsa_w_out,
           ffn_w_up, ffn_conv_w, ffn_conv_b, ffn_w_down):
    B, T, D = x_prompt.shape
    Bs, Tq, _ = x_sample.shape
    depth = norm_mix.shape[0]
    Fd = ffn_w_down.shape[1]
    G, P = ssm_a_re.shape[1:]
    xp = x_prompt
    xs = x_sample.transpose(1, 0, 2).reshape(1, Tq * Bs, D)
    p_sre, p_sim, s_sre, s_sim, p_cv, s_cv = [], [], [], [], [], []
    p_rows = [[] for _ in range(N_KINDS)]
    s_rows = [[] for _ in range(N_KINDS)]
    for i in range(depth):
        li = i // 2
        if i % 2 == 0:
            tabs = _s5_tables(ssm_a_re[li], ssm_a_im[li], ssm_log_dt[li], ssm_b_re[li], ssm_b_im[li],
                              ssm_c_re[li], ssm_c_im[li])
            sp = (norm_mix[i], ssm_w_in[li], tabs, ssm_d[li], ssm_w_gate[li], ssm_w_out[li])
            z = jnp.zeros((B, SUBLANES, G * P), F32)
            xp, fr, fi = _s5_layer(xp, z, z, *sp, tc=min(S5_CHUNK, T), slab=1)
            p_sre.append(fr[:, SUBLANES - 1].reshape(B, G, P))
            p_sim.append(fi[:, SUBLANES - 1].reshape(B, G, P))
            xs, fr, fi = _s5_layer(xs, state_ssm_re[li].reshape(1, Bs, G * P).astype(F32),
                                   state_ssm_im[li].reshape(1, Bs, G * P).astype(F32), *sp, tc=Tq * Bs, slab=Bs)
            s_sre.append(fr.reshape(Bs, G, P))
            s_sim.append(fi.reshape(Bs, G, P))
        else:
            ap = (norm_mix[i], nsa_w_in[li], nsa_q_gain[li], nsa_k_gain[li],
                  (nsa_ck_w1[li], nsa_ck_b1[li], nsa_ck_w2[li]), (nsa_cv_w1[li], nsa_cv_b1[li], nsa_cv_w2[li]),
                  nsa_w_out[li])
            xp, rp = _nsa_prompt_layer(xp, *ap)
            xs, rs = _nsa_sample_layer(xs, Bs, (cache_k_cmp[li], cache_v_cmp[li], cache_k_slc[li], cache_v_slc[li]),
                                       cache_k_win[li], cache_v_win[li], page_table, *ap)
            for j in range(N_KINDS):
                p_rows[j].append(rp[j])
                s_rows[j].append(rs[j])
        fw = (norm_ffn[i], ffn_w_up[i], ffn_conv_w[i], ffn_conv_b[i], ffn_w_down[i])
        xp, cp = _conv_ffn(xp, jnp.zeros((B, SUBLANES, Fd), F32), *fw, shift=1, tm=min(PROMPT_ROW_TILE, T))
        p_cv.append(cp[:, SUBLANES - (CONV_W - 1):])
        init = state_conv[i].astype(F32).transpose(1, 0, 2).reshape(1, (CONV_W - 1) * Bs, Fd)
        xs, cs = _conv_ffn(xs, init, *fw, shift=Bs, tm=Tq * Bs)
        s_cv.append(cs.reshape(CONV_W - 1, Bs, Fd).transpose(1, 0, 2))
    st = jnp.stack
    y_sample = xs.reshape(Tq, Bs, D).transpose(1, 0, 2)
    return (xp, y_sample,
            st(p_sre), st(p_sim), *[st(r) for r in p_rows], st(p_cv),
            st(s_sre), st(s_sim), *[st(r) for r in s_rows], st(s_cv))
```

```python
import functools
import math

import jax
import jax.numpy as jnp
from jax import lax
from jax.experimental import pallas as pl
from jax.experimental.pallas import tpu as pltpu

F32 = jnp.float32
BF16 = jnp.bfloat16

NORM_EPS = 1e-6
SSM_CH = 16
SSM_STATE = 64
N_HEADS = 16
N_KV_HEADS = 4
HEAD_DIM = 64
L_CMP = 32
L_SEL = 64
TOP_N = 16
WINDOW = 512
PAGE_SIZE = 128
ROPE_THETA = 10000.0
NEG_INF = -1e30
FORCE_SCORE = 1e4
CONV_W = 3

LANES = 128
SUBLANES = 8
VMEM_LIMIT = 56 * 1024 * 1024


def _cparams(sem):
    return pltpu.CompilerParams(dimension_semantics=sem, vmem_limit_bytes=VMEM_LIMIT)


def _rms(x, g):
    return x * lax.rsqrt(jnp.mean(x * x, axis=-1, keepdims=True) + NORM_EPS) * g


FFN_MIN_SUBTILE = 256
FFN_CHUNKS = 1


def _ffn_kernel(x_ref, g_ref, wa_ref, wb_ref, cw_ref, cb_ref, wd_ref, init_ref,
                o_ref, buf_ref, xn_s, acc_s, aext_s, carry_s, *, shift, base, tm, nf, nsub):
    i = pl.program_id(1)
    j = pl.program_id(2)

    @pl.when(j == 0)
    def _():
        x = x_ref[...]
        xn_s[...] = _rms(x, g_ref[...]).astype(BF16)
        acc_s[...] = x

    @pl.when(i == 0)
    def _():
        aext_s[0:base, :] = init_ref[...]

    @pl.when(i > 0)
    def _():
        aext_s[0:base, :] = carry_s[j]

    cw = cw_ref[...]
    ts = tm // nsub
    for sub in range(nsub):
        r0 = sub * ts
        xn = xn_s[r0:r0 + ts, :]
        a = jnp.dot(xn, wa_ref[...], preferred_element_type=F32)
        b = jnp.dot(xn, wb_ref[...], preferred_element_type=F32)
        aext_s[base + r0:base + r0 + ts, :] = a
        a1 = aext_s[base - shift + r0:base - shift + r0 + ts, :]
        a2 = aext_s[base - 2 * shift + r0:base - 2 * shift + r0 + ts, :]
        c = cb_ref[...] + cw[0:1, :] * a2
        c = c + cw[1:2, :] * a1
        c = c + cw[2:3, :] * a
        h = (jax.nn.silu(c) * b).astype(BF16)
        acc_s[r0:r0 + ts, :] += jnp.dot(h, wd_ref[...], preferred_element_type=F32)
    tail = aext_s[tm:tm + base, :]
    carry_s[j] = tail
    buf_ref[...] = tail

    @pl.when(j == nf - 1)
    def _():
        o_ref[...] = acc_s[...]


def _conv_ffn(x, init, g, w_up, conv_w, conv_b, w_down, *, shift, tm):
    S, R, D = x.shape
    Fd = w_down.shape[0]
    base = init.shape[1]
    nf = FFN_CHUNKS
    tf = Fd // nf
    assert tf * nf == Fd and tf % LANES == 0 and R % tm == 0 and base >= 2 * shift
    wu = w_up.astype(BF16)
    wd = w_down.astype(BF16)
    nsub = 2 if tm % (2 * FFN_MIN_SUBTILE) == 0 else 1
    kern = functools.partial(_ffn_kernel, shift=shift, base=base, tm=tm, nf=nf, nsub=nsub)
    wmode = pl.Buffered(1) if nf == 1 else None
    out, buf = pl.pallas_call(
        kern,
        grid=(S, R // tm, nf),
        in_specs=[
            pl.BlockSpec((None, tm, D), lambda s, i, j: (s, i, 0)),
            pl.BlockSpec((1, D), lambda s, i, j: (0, 0)),
            pl.BlockSpec((D, tf), lambda s, i, j: (0, j), pipeline_mode=wmode),
            pl.BlockSpec((D, tf), lambda s, i, j: (0, nf + j), pipeline_mode=wmode),
            pl.BlockSpec((CONV_W, tf), lambda s, i, j: (0, j)),
            pl.BlockSpec((1, tf), lambda s, i, j: (0, j)),
            pl.BlockSpec((tf, D), lambda s, i, j: (j, 0), pipeline_mode=wmode),
            pl.BlockSpec((None, base, tf), lambda s, i, j: (s, 0, j)),
        ],
        out_specs=[
            pl.BlockSpec((None, tm, D), lambda s, i, j: (s, i, 0)),
            pl.BlockSpec((None, None, base, tf), lambda s, i, j: (s, i, 0, j)),
        ],
        out_shape=[jax.ShapeDtypeStruct((S, R, D), F32), jax.ShapeDtypeStruct((S, R // tm, base, Fd), F32)],
        scratch_shapes=[
            pltpu.VMEM((tm, D), BF16),
            pltpu.VMEM((tm, D), F32),
            pltpu.VMEM((tm + base, tf), F32),
            pltpu.VMEM((nf, base, tf), F32),
        ],
        compiler_params=_cparams(("arbitrary", "arbitrary", "arbitrary")),
        name="conv_ffn",
    )(x, g.reshape(1, D), wu, wu, conv_w, conv_b.reshape(1, Fd), wd, init)
    return out, buf[:, R // tm - 1]


def _s5_tables(a_re, a_im, log_dt, b_re, b_im, c_re, c_im):
    G, P = a_re.shape
    nt = G * SSM_CH // LANES
    gt = G // nt
    a_re, a_im = a_re.astype(F32), a_im.astype(F32)
    dt = jnp.exp(log_dt.astype(F32))[:, None]
    mag = jnp.exp(a_re * dt)
    lam_re, lam_im = mag * jnp.cos(a_im * dt), mag * jnp.sin(a_im * dt)
    den = a_re * a_re + a_im * a_im
    n_re, n_im = lam_re - 1.0, lam_im
    coef_re = (n_re * a_re + n_im * a_im) / den
    coef_im = (n_im * a_re - n_re * a_im) / den
    b_re, b_im = b_re.astype(F32), b_im.astype(F32)
    bb_re = coef_re[..., None] * b_re - coef_im[..., None] * b_im
    bb_im = coef_re[..., None] * b_im + coef_im[..., None] * b_re
    eye = jnp.eye(gt, dtype=F32)

    def in_proj(bb):
        v = bb.reshape(nt, gt, P, SSM_CH).transpose(0, 1, 3, 2)
        return jnp.einsum('igcp,gh->igchp', v, eye).reshape(nt, gt * SSM_CH, gt * P)

    def out_proj(c):
        v = c.astype(F32).reshape(nt, gt, SSM_CH, P)
        return jnp.einsum('igcp,gh->igphc', v, eye).reshape(nt, gt * P, gt * SSM_CH)

    def cmul(xr, xi, yr, yi):
        return xr * yr - xi * yi, xr * yi + xi * yr

    pows = [(jnp.ones_like(lam_re), jnp.zeros_like(lam_im))]
    for _ in range(SUBLANES):
        pows.append(cmul(*pows[-1], lam_re, lam_im))

    def in_proj_pair(j):
        pr, pi = pows[j][0][..., None], pows[j][1][..., None]
        return jnp.concatenate([in_proj(pr * bb_re - pi * bb_im), in_proj(pr * bb_im + pi * bb_re)], axis=2)

    wb = in_proj_pair(0).astype(BF16)
    wb_fir = jnp.concatenate([in_proj_pair(j) for j in range(SUBLANES)], axis=1).astype(BF16)
    wc = jnp.concatenate([out_proj(c_re), -out_proj(c_im)], axis=1).astype(BF16)
    flat = lambda x: x.reshape(1, G * P)
    rep = lambda x: jnp.broadcast_to(flat(x), (SUBLANES, G * P))
    lam = jnp.concatenate([flat(pows[1][0]), flat(pows[1][1]), jnp.zeros((SUBLANES - 2, G * P), F32)], axis=0)
    lp_re = jnp.concatenate([flat(p[0]) for p in pows[1:]], axis=0)
    lp_im = jnp.concatenate([flat(p[1]) for p in pows[1:]], axis=0)
    return dict(wb=wb, wb_fir=wb_fir, wc=wc, lam=lam, lp_re=lp_re, lp_im=lp_im,
                l8_re=rep(pows[SUBLANES][0]), l8_im=rep(pows[SUBLANES][1]))


def _s5_kernel(x_ref, g_ref, win_ref, wb_ref, wc_ref, lam_ref, lpr_ref, lpi_ref, l8r_ref, l8i_ref, d_ref, wg_ref,
               wo_ref, ir_ref, ii_ref, o_ref, fr_ref, fi_ref, sre, sim, uext, y_s, *, tc, slab, nt):
    ci = pl.program_id(1)
    hist = SUBLANES
    x = x_ref[...]
    xn = _rms(x, g_ref[...]).astype(BF16)
    uext[hist:hist + tc, :] = jnp.dot(xn, win_ref[...], preferred_element_type=F32)
    n2 = sre.shape[1] // nt

    if slab == 1:
        @pl.when(ci == 0)
        def _():
            uext[0:hist, :] = jnp.zeros((hist, uext.shape[1]), F32)
            fr_ref[...] = ir_ref[...]
            fi_ref[...] = ii_ref[...]

        first = ci == 0
        for i in range(nt):
            ls = slice(i * LANES, (i + 1) * LANES)
            cs = slice(i * n2, (i + 1) * n2)
            taps = jnp.concatenate([uext[hist - j:hist - j + tc, ls].astype(BF16) for j in range(SUBLANES)], axis=1)
            v = jnp.dot(taps, wb_ref[i], preferred_element_type=F32)
            l8r, l8i = l8r_ref[:, cs], l8i_ref[:, cs]
            pr, pi = fr_ref[:, cs], fi_ref[:, cs]
            h0r, h0i = pr[SUBLANES - 1:SUBLANES, :], pi[SUBLANES - 1:SUBLANES, :]
            lpr, lpi = lpr_ref[:, cs], lpi_ref[:, cs]
            hr = jnp.where(first, lpr * h0r - lpi * h0i, l8r * pr - l8i * pi) + v[0:SUBLANES, :n2]
            hi = jnp.where(first, lpr * h0i + lpi * h0r, l8r * pi + l8i * pr) + v[0:SUBLANES, n2:]
            slabs_r, slabs_i = [hr], [hi]
            for k in range(1, tc // SUBLANES):
                rs = slice(k * SUBLANES, (k + 1) * SUBLANES)
                hr, hi = l8r * hr - l8i * hi + v[rs, :n2], l8r * hi + l8i * hr + v[rs, n2:]
                slabs_r.append(hr)
                slabs_i.append(hi)
            sre[:, cs] = jnp.concatenate(slabs_r, axis=0)
            sim[:, cs] = jnp.concatenate(slabs_i, axis=0)
        uext[0:hist, :] = uext[tc:tc + hist, :]
        fr_ref[...] = sre[tc - SUBLANES:tc, :]
        fi_ref[...] = sim[tc - SUBLANES:tc, :]
    else:
        ub = uext[hist:hist + tc, :].astype(BF16)
        for i in range(nt):
            bu = jnp.dot(ub[:, i * LANES:(i + 1) * LANES], wb_ref[i], preferred_element_type=F32)
            sre[:, i * n2:(i + 1) * n2] = bu[:, :n2]
            sim[:, i * n2:(i + 1) * n2] = bu[:, n2:]
        lr, li = lam_ref[0:1, :], lam_ref[1:2, :]
        hr, hi = ir_ref[...], ii_ref[...]
        for t in range(tc // slab):
            rs = slice(t * slab, (t + 1) * slab)
            hr, hi = lr * hr - li * hi + sre[rs, :], lr * hi + li * hr + sim[rs, :]
            sre[rs, :] = hr
            sim[rs, :] = hi
        fr_ref[...] = hr
        fi_ref[...] = hi

    for i in range(nt):
        cs = slice(i * n2, (i + 1) * n2)
        s_cat = jnp.concatenate([sre[:, cs].astype(BF16), sim[:, cs].astype(BF16)], axis=1)
        y_s[:, i * LANES:(i + 1) * LANES] = jnp.dot(s_cat, wc_ref[i], preferred_element_type=F32)
    z = jax.nn.gelu(y_s[...] + d_ref[...] * uext[hist:hist + tc, :])
    gate = jnp.dot(z.astype(BF16), wg_ref[...], preferred_element_type=F32)
    out = jnp.dot((z * jax.nn.sigmoid(gate)).astype(BF16), wo_ref[...], preferred_element_type=F32)
    o_ref[...] = x_ref[...] + out


def _s5_layer(x, init_re, init_im, g, w_in, tables, d, w_gate, w_out, *, tc, slab):
    S, R, D = x.shape
    wb = tables["wb_fir"] if slab == 1 else tables["wb"]
    wc = tables["wc"]
    nt = wb.shape[0]
    N = tables["lam"].shape[1]
    hb = init_re.shape[1]
    assert R % tc == 0 and tc % SUBLANES == 0 and (slab == 1 or R == tc)
    kern = functools.partial(_s5_kernel, tc=tc, slab=slab, nt=nt)
    const = lambda shape: pl.BlockSpec(shape, lambda s, c: (0,) * len(shape), pipeline_mode=pl.Buffered(1))
    out, fr, fi = pl.pallas_call(
        kern,
        grid=(S, R // tc),
        in_specs=[
            pl.BlockSpec((None, tc, D), lambda s, c: (s, c, 0)),
            const((1, D)),
            const((D, D)),
            const(wb.shape),
            const(wc.shape),
            const((SUBLANES, N)), const((SUBLANES, N)), const((SUBLANES, N)), const((SUBLANES, N)),
            const((SUBLANES, N)),
            const((1, D)),
            const((D, D)),
            const((D, D)),
            pl.BlockSpec((None, hb, N), lambda s, c: (s, 0, 0)),
            pl.BlockSpec((None, hb, N), lambda s, c: (s, 0, 0)),
        ],
        out_specs=[
            pl.BlockSpec((None, tc, D), lambda s, c: (s, c, 0)),
            pl.BlockSpec((None, hb, N), lambda s, c: (s, 0, 0)),
            pl.BlockSpec((None, hb, N), lambda s, c: (s, 0, 0)),
        ],
        out_shape=[jax.ShapeDtypeStruct((S, R, D), F32), jax.ShapeDtypeStruct((S, hb, N), F32),
                   jax.ShapeDtypeStruct((S, hb, N), F32)],
        scratch_shapes=[pltpu.VMEM((tc, N), F32), pltpu.VMEM((tc, N), F32),
                        pltpu.VMEM((SUBLANES + tc, D), F32), pltpu.VMEM((tc, D), F32)],
        compiler_params=_cparams(("arbitrary", "arbitrary")),
        name="s5_layer",
    )(x, g.reshape(1, D), w_in.astype(BF16), wb, wc, tables["lam"], tables["lp_re"], tables["lp_im"],
      tables["l8_re"], tables["l8_im"], d.reshape(1, D).astype(F32),
      w_gate.astype(BF16), w_out.astype(BF16), init_re, init_im)
    return out, fr, fi


D_Q = N_HEADS * HEAD_DIM
KV_DIM = N_KV_HEADS * HEAD_DIM
N_KINDS = 6
N_GATES = 3 * N_HEADS
HALF = HEAD_DIM // 2


def _rope_tables(pos):
    inv = jnp.power(ROPE_THETA, -jnp.arange(HALF, dtype=F32) / HALF)
    ang = inv[:, None] * pos.astype(F32)[None, :]
    return jnp.cos(ang), jnp.sin(ang)


def _nsa_proj_kernel(x_ref, g_ref, wt_ref, qg_ref, kg_ref, cos_ref, sin_ref, q_ref, gt_ref, *rest, tm, paged):
    if paged:
        kind_refs, (ksa_ref, vsa_ref, kw_ref, vwa_ref, kcp_ref, vcp_ref, pt_s) = rest[:N_KINDS], rest[N_KINDS:]
        n_sel = ksa_ref.shape[1] - HEAD_DIM
        key = pl.program_id(1) * tm + lax.broadcasted_iota(jnp.int32, (n_sel, tm), 1)
        expand = jnp.where(key // L_SEL == lax.broadcasted_iota(jnp.int32, (n_sel, tm), 0), 1.0, 0.0).astype(BF16)
        ones_row = jnp.where(lax.broadcasted_iota(jnp.int32, (BF16_SUBLANES, tm), 0) == 0, 1.0, 0.0).astype(BF16)
    else:
        kv_ref, pt_s = rest
    xn = _rms(x_ref[...], g_ref[...]).astype(BF16)
    pt_s[...] = lax.dot_general(wt_ref[...], xn, (((1,), (1,)), ((), ())), preferred_element_type=F32)
    cos, sin = cos_ref[...], sin_ref[...]

    def norm_rope(blk, gain):
        y = blk * lax.rsqrt(jnp.mean(blk * blk, axis=0, keepdims=True) + NORM_EPS) * gain
        x1, x2 = y[:HALF], y[HALF:]
        return x1 * cos - x2 * sin, x2 * cos + x1 * sin

    scale = HEAD_DIM ** -0.5
    for h in range(N_HEADS):
        r0 = h * HEAD_DIM
        r1, r2 = norm_rope(pt_s[r0:r0 + HEAD_DIM, :], qg_ref[...])
        q_ref[r0:r0 + HALF, :] = (r1 * scale).astype(BF16)
        q_ref[r0 + HALF:r0 + HEAD_DIM, :] = (r2 * scale).astype(BF16)
    for kind in range(N_KINDS):
        for gi in range(N_KV_HEADS):
            r0 = gi * HEAD_DIM
            blk = pt_s[D_Q + kind * KV_DIM + r0:D_Q + kind * KV_DIM + r0 + HEAD_DIM, :]
            if kind % 2 == 0:
                r1, r2 = norm_rope(blk, kg_ref[kind // 2])
                blk = jnp.concatenate([r1, r2], axis=0)
            if not paged:
                kv_ref[kind, r0:r0 + HEAD_DIM, :] = blk
                continue
            kind_refs[kind][r0:r0 + HEAD_DIM, :] = blk
            if kind < 2:
                dst = kcp_ref if kind == 0 else vcp_ref
                for p in range(tm // PAGE_SIZE):
                    dst[p, gi] = blk[:, p * PAGE_SIZE:(p + 1) * PAGE_SIZE]
            elif kind == 2:
                ksa_ref[gi, 0:n_sel, :] = expand
                ksa_ref[gi, n_sel:n_sel + HEAD_DIM, :] = blk.astype(BF16)
            elif kind == 4:
                kw_ref[gi] = blk.astype(BF16)
            else:
                dst = vsa_ref if kind == 3 else vwa_ref
                dst[gi, 0:HEAD_DIM, :] = blk.astype(BF16)
                dst[gi, HEAD_DIM:HEAD_DIM + BF16_SUBLANES, :] = ones_row
    g0 = D_Q + N_KINDS * KV_DIM
    gt_ref[...] = jax.nn.sigmoid(pt_s[g0:g0 + N_GATES, :])


def _nsa_project(x, pos, g, w_in, q_gain, k_gain, *, tm, paged):
    S, R, D = x.shape
    NP = w_in.shape[1]
    G = N_KV_HEADS
    assert R % tm == 0 and (not paged or tm % PAGE_SIZE == 0)
    nr = R // tm
    wt = w_in.T.astype(BF16)
    cos, sin = _rope_tables(pos)
    qg = jnp.broadcast_to(q_gain.astype(F32)[:, None], (HEAD_DIM, tm))
    kg = jnp.broadcast_to(k_gain.astype(F32)[:, :, None], (3, HEAD_DIM, tm))
    out_specs = [
        pl.BlockSpec((None, D_Q, tm), lambda s, i: (s, 0, i)),
        pl.BlockSpec((None, N_GATES, tm), lambda s, i: (s, 0, i)),
    ]
    out_shape = [jax.ShapeDtypeStruct((S, D_Q, R), BF16), jax.ShapeDtypeStruct((S, N_GATES, R), F32)]
    if paged:
        ppt = tm // PAGE_SIZE
        va = HEAD_DIM + BF16_SUBLANES
        grouped = lambda rows: pl.BlockSpec((None, G, rows, tm), lambda s, i: (s, 0, 0, i))
        out_specs += [pl.BlockSpec((None, KV_DIM, tm), lambda s, i: (s, 0, i))] * N_KINDS
        out_shape += [jax.ShapeDtypeStruct((S, KV_DIM, R), F32)] * N_KINDS
        for rows in (R // L_SEL + HEAD_DIM, va, HEAD_DIM, va):
            out_specs.append(grouped(rows))
            out_shape.append(jax.ShapeDtypeStruct((S, G, rows, R), BF16))
        for _ in range(2):
            out_specs.append(pl.BlockSpec((ppt, G, HEAD_DIM, PAGE_SIZE), lambda s, i: (s * nr + i, 0, 0, 0)))
            out_shape.append(jax.ShapeDtypeStruct((S * R // PAGE_SIZE, G, HEAD_DIM, PAGE_SIZE), F32))
    else:
        out_specs.append(pl.BlockSpec((None, N_KINDS, KV_DIM, tm), lambda s, i: (s, 0, 0, i)))
        out_shape.append(jax.ShapeDtypeStruct((S, N_KINDS, KV_DIM, R), F32))
    return pl.pallas_call(
        functools.partial(_nsa_proj_kernel, tm=tm, paged=paged),
        grid=(S, nr),
        in_specs=[
            pl.BlockSpec((None, tm, D), lambda s, i: (s, i, 0)),
            pl.BlockSpec((1, D), lambda s, i: (0, 0)),
            pl.BlockSpec((NP, D), lambda s, i: (0, 0)),
            pl.BlockSpec((HEAD_DIM, tm), lambda s, i: (0, 0)),
            pl.BlockSpec((3, HEAD_DIM, tm), lambda s, i: (0, 0, 0)),
            pl.BlockSpec((HALF, tm), lambda s, i: (0, i)),
            pl.BlockSpec((HALF, tm), lambda s, i: (0, i)),
        ],
        out_specs=out_specs,
        out_shape=out_shape,
        scratch_shapes=[pltpu.VMEM((NP, tm), F32)],
        compiler_params=_cparams(("arbitrary", "arbitrary")),
        name="nsa_project",
    )(x, g.reshape(1, D), wt, qg, kg, cos, sin)


PAGE_ROWS = N_KV_HEADS * HEAD_DIM
CMP_PER_PAGE = PAGE_SIZE // L_CMP
DMA_LOOP_UNROLL = 4


def _page_copies(pages_hbm, buf, sem, page, slot, p, by_feature):
    if not by_feature:
        return [pltpu.make_async_copy(pages_hbm.at[page], buf.at[slot, pl.ds(p * PAGE_ROWS, PAGE_ROWS), :],
                                      sem.at[slot])]
    return [pltpu.make_async_copy(pages_hbm.at[page, pl.ds(g * HEAD_DIM, HEAD_DIM), :],
                                  buf.at[slot, :, p * N_KV_HEADS + g, :], sem.at[slot])
            for g in range(N_KV_HEADS)]


def _fetch_chunk(pt_ref, pages_hbm, buf, sem, n, slot, *, nchunks, npc, by_feature):
    s = n // nchunks
    c = n % nchunks

    def start(p, carry):
        for cp in _page_copies(pages_hbm, buf, sem, pt_ref[s, c * npc + p], slot, p, by_feature):
            cp.start()
        return carry

    lax.fori_loop(0, npc, start, 0, unroll=DMA_LOOP_UNROLL)


def _wait_chunk(pages_hbm, buf, sem, slot, *, npc, by_feature):
    def wait(p, carry):
        for cp in _page_copies(pages_hbm, buf, sem, 0, slot, p, by_feature):
            cp.wait()
        return carry

    lax.fori_loop(0, npc, wait, 0, unroll=DMA_LOOP_UNROLL)


def _stream_pages(pt_ref, pages_hbm, buf, sem, *, nchunks, npc, total, by_feature=False):
    n = pl.program_id(0) * nchunks + pl.program_id(1)
    slot = n % 2
    kw = dict(nchunks=nchunks, npc=npc, by_feature=by_feature)

    @pl.when(n == 0)
    def _():
        _fetch_chunk(pt_ref, pages_hbm, buf, sem, n, slot, **kw)

    @pl.when(n + 1 < total)
    def _():
        _fetch_chunk(pt_ref, pages_hbm, buf, sem, n + 1, 1 - slot, **kw)

    _wait_chunk(pages_hbm, buf, sem, slot, npc=npc, by_feature=by_feature)
    return slot


def _compress_tables(w1, b1, w2):
    eye = jnp.eye(CMP_PER_PAGE, dtype=F32)
    m = jnp.einsum('lde,ck->dclke', w1.astype(F32), eye)
    m = m.reshape(HEAD_DIM * PAGE_SIZE, CMP_PER_PAGE * HEAD_DIM).astype(BF16)
    b1t = jnp.tile(b1.astype(F32), CMP_PER_PAGE).reshape(1, CMP_PER_PAGE * HEAD_DIM)
    w2bd = jnp.einsum('ef,ck->cekf', w2.astype(F32), eye)
    w2bd = w2bd.reshape(CMP_PER_PAGE * HEAD_DIM, CMP_PER_PAGE * HEAD_DIM).astype(BF16)
    return m, b1t, w2bd


def _compress_kernel(pt_ref, pages_hbm, m_ref, b1_ref, w2_ref, o_ref, buf, sem, lhs_s, *, nchunks, npc, total):
    slot = _stream_pages(pt_ref, pages_hbm, buf, sem, nchunks=nchunks, npc=npc, total=total, by_feature=True)
    for d in range(HEAD_DIM):
        lhs_s[:, d * PAGE_SIZE:(d + 1) * PAGE_SIZE] = buf[slot, d].astype(BF16)
    acc = jnp.dot(lhs_s[...], m_ref[...], preferred_element_type=F32)
    h = jax.nn.gelu(acc + b1_ref[...])
    o_ref[...] = jnp.dot(h.astype(BF16), w2_ref[...], preferred_element_type=F32)


def _compress(pages, page_table, tables, *, npc):
    S, NPG = page_table.shape
    assert NPG % npc == 0
    nchunks = NPG // npc
    m, b1t, w2bd = tables
    nrow = npc * N_KV_HEADS
    nce = CMP_PER_PAGE * HEAD_DIM
    pages2 = pages.reshape(pages.shape[0], PAGE_ROWS, PAGE_SIZE)
    kern = functools.partial(_compress_kernel, nchunks=nchunks, npc=npc, total=S * nchunks)
    out = pl.pallas_call(
        kern,
        grid_spec=pltpu.PrefetchScalarGridSpec(
            num_scalar_prefetch=1,
            grid=(S, nchunks),
            in_specs=[
                pl.BlockSpec(memory_space=pl.ANY),
                pl.BlockSpec(m.shape, lambda s, c, pt: (0, 0)),
                pl.BlockSpec(b1t.shape, lambda s, c, pt: (0, 0)),
                pl.BlockSpec(w2bd.shape, lambda s, c, pt: (0, 0)),
            ],
            out_specs=pl.BlockSpec((None, nrow, nce), lambda s, c, pt: (s, c, 0)),
            scratch_shapes=[pltpu.VMEM((2, HEAD_DIM, nrow, PAGE_SIZE), F32), pltpu.SemaphoreType.DMA((2,)),
                            pltpu.VMEM((nrow, HEAD_DIM * PAGE_SIZE), BF16)],
        ),
        out_shape=jax.ShapeDtypeStruct((S, NPG * N_KV_HEADS, nce), F32),
        compiler_params=_cparams(("arbitrary", "arbitrary")),
        name="nsa_compress",
    )(page_table, pages2, m, b1t, w2bd)
    out = out.reshape(S, NPG, N_KV_HEADS, CMP_PER_PAGE, HEAD_DIM).transpose(0, 1, 3, 2, 4)
    return out.reshape(S, NPG * CMP_PER_PAGE, N_KV_HEADS, HEAD_DIM)


HPG = N_HEADS // N_KV_HEADS
CMP_PER_SEL = L_SEL // L_CMP
NT_DIMS = (((1,), (1,)), ((), ()))


def _masked_softmax(s, mask):
    s = jnp.where(mask, s, NEG_INF)
    e = jnp.exp(s - jnp.max(s, axis=-1, keepdims=True))
    return jnp.where(mask, e / jnp.sum(e, axis=-1, keepdims=True), 0.0)


def _select_blocks(imp, pos_q, n_sel):
    nq = imp.shape[0]
    jidx = lax.broadcasted_iota(jnp.int32, (nq, n_sel), 1)
    qblk = pos_q // L_SEL
    valid = jidx * L_SEL <= pos_q
    forced = (jidx == 0) | (jidx == qblk) | (jidx == qblk - 1)
    score = jnp.where(valid & forced, FORCE_SCORE, jnp.where(valid, imp, NEG_INF))
    sct = score.T
    jrow = lax.broadcasted_iota(jnp.int32, (n_sel, nq), 0)
    sel = jnp.zeros((n_sel, nq), F32)
    for _ in range(min(TOP_N, n_sel)):
        top = jnp.max(sct, axis=0, keepdims=True)
        first = jnp.min(jnp.where(sct == top, jrow, n_sel), axis=0, keepdims=True)
        hit = jrow == first
        sct = jnp.where(hit, -jnp.inf, sct)
        sel = jnp.where(hit, 1.0, sel)
    return sel


def _cmp_block_of_column(ncol):
    c = lax.broadcasted_iota(jnp.int32, (1, ncol), 1)
    half = ncol // CMP_PER_SEL
    return jnp.where(c < half, CMP_PER_SEL * c, CMP_PER_SEL * (c - half) + 1)


Q_BLOCK = 128
GROUPS_PER_STEP = 4
BF16_SUBLANES = 2 * SUBLANES


M_FLOOR = 0.1 * NEG_INF


def _bias_softmax(s, bias):
    s = s + bias
    e = jnp.exp(s - jnp.maximum(jnp.max(s, axis=-1, keepdims=True), M_FLOOR))
    l = jnp.sum(e, axis=-1, keepdims=True)
    return e, 1.0 / jnp.where(l > 0.0, l, 1.0)


def _attn_prompt_kernel(q_ref, kc_ref, vc_ref, ksa_ref, vsa_ref, kw_ref, vwa_ref, gt_ref, o_ref, *, tk, T, gb):
    qi = pl.program_id(2)
    nq = Q_BLOCK
    nr = HPG * nq
    nc = kc_ref.shape[2]
    n_sel = nc // CMP_PER_SEL
    va = vsa_ref.shape[1]
    pos_q = qi * nq + lax.broadcasted_iota(jnp.int32, (nq, 1), 0)
    cbias = jnp.where((_cmp_block_of_column(nc) * L_CMP + (L_CMP - 1)) <= pos_q, 0.0, NEG_INF)

    def normalise(r):
        l = r[..., HEAD_DIM:HEAD_DIM + 1]
        return r[..., :HEAD_DIM] * (1.0 / jnp.where(l > 0.0, l, 1.0))

    qs, q_augs, o_cs = [], [], []
    for gl in range(gb):
        qrows = q_ref[gl].astype(F32).T
        q = jnp.concatenate([qrows[:, h * HEAD_DIM:(h + 1) * HEAD_DIM] for h in range(HPG)], axis=0)
        q = q.astype(BF16)
        s_c = jnp.dot(q, kc_ref[gl], preferred_element_type=F32).reshape(HPG, nq, nc)
        e_c, inv_c = _bias_softmax(s_c, cbias[None])
        p_c = e_c * inv_c
        o_cs.append(jnp.dot(p_c.reshape(nr, nc).astype(BF16), vc_ref[gl],
                            preferred_element_type=F32).reshape(HPG, nq, HEAD_DIM))
        imp = p_c[0]
        for h in range(1, HPG):
            imp = imp + p_c[h]
        imp = imp[:, :n_sel] + imp[:, n_sel:]
        sel = _select_blocks(imp, pos_q, n_sel).T
        unsel = jnp.where(sel > 0.5, 0.0, NEG_INF).astype(BF16)
        qs.append(q)
        q_augs.append(jnp.concatenate([jnp.concatenate([unsel] * HPG, axis=0), q], axis=1))

    def sel_tile(gl, k0, carry, causal):
        m, acc = carry
        s = jnp.dot(q_augs[gl], ksa_ref[gl, :, pl.ds(k0, tk)], preferred_element_type=F32)
        s = s.reshape(HPG, nq, tk)
        if causal:
            kpos = k0 + lax.broadcasted_iota(jnp.int32, (nq, tk), 1)
            s = s + jnp.where(kpos <= pos_q, 0.0, NEG_INF)[None]
        m_new = jnp.maximum(m, jnp.max(s, axis=-1, keepdims=True))
        p = jnp.exp(s - m_new).astype(BF16)
        pv = lax.dot_general(p.reshape(nr, tk), vsa_ref[gl, :, pl.ds(k0, tk)], NT_DIMS,
                             preferred_element_type=F32)
        return m_new, jnp.exp(m - m_new) * acc + pv.reshape(HPG, nq, va)

    def sel_tiles(k0, carries, causal):
        return tuple(sel_tile(gl, k0, carries[gl], causal) for gl in range(gb))

    last = (qi * nq + nq + tk - 1) // tk - 1
    init = (jnp.full((HPG, nq, 1), M_FLOOR, F32), jnp.zeros((HPG, nq, va), F32))
    carries = lax.fori_loop(0, last, lambda kt, c: sel_tiles(pl.multiple_of(kt * tk, tk), c, False),
                            (init,) * gb)
    carries = sel_tiles(pl.multiple_of(last * tk, tk), carries, True)

    wl = min(WINDOW + nq, T)
    w0 = pl.multiple_of(jnp.maximum(qi * nq + nq - wl, 0), LANES)
    dpos = pos_q - (w0 + lax.broadcasted_iota(jnp.int32, (nq, wl), 1))
    wbias = jnp.where((dpos >= 0) & (dpos < WINDOW), 0.0, NEG_INF)[None]
    for gl in range(gb):
        o_s = normalise(carries[gl][1])
        s_w = jnp.dot(qs[gl], kw_ref[gl, :, pl.ds(w0, wl)], preferred_element_type=F32).reshape(HPG, nq, wl)
        s_w = s_w + wbias
        e_w = jnp.exp(s_w - jnp.maximum(jnp.max(s_w, axis=-1, keepdims=True), M_FLOOR)).astype(BF16)
        o_w = normalise(lax.dot_general(e_w.reshape(nr, wl), vwa_ref[gl, :, pl.ds(w0, wl)], NT_DIMS,
                                        preferred_element_type=F32).reshape(HPG, nq, va))
        gt = gt_ref[gl]
        o_c = o_cs[gl]
        outs = []
        for h in range(HPG):
            outs.append(gt[:, h:h + 1] * o_c[h] + gt[:, HPG + h:HPG + h + 1] * o_s[h]
                        + gt[:, 2 * HPG + h:2 * HPG + h + 1] * o_w[h])
        o_ref[:, gl * HPG * HEAD_DIM:(gl + 1) * HPG * HEAD_DIM] = jnp.concatenate(outs, axis=1).astype(BF16)


def _block_expansion(n_sel, n_keys):
    return (jnp.arange(n_keys)[None, :] // L_SEL == jnp.arange(n_sel)[:, None]).astype(BF16)


def _cmp_layouts(kc, vc):
    order = jnp.concatenate([jnp.arange(0, kc.shape[1], CMP_PER_SEL), jnp.arange(1, kc.shape[1], CMP_PER_SEL)])
    kct = kc[:, order].transpose(0, 2, 3, 1).astype(BF16)
    vcr = vc[:, order].transpose(0, 2, 1, 3).astype(BF16)
    return kct, vcr


def _attend_prompt(qT, ks_aug, vs_aug, kw, vw_aug, gT, kc, vc, *, tk):
    B, _, T = qT.shape
    G = N_KV_HEADS
    nq = Q_BLOCK
    assert T % tk == 0 and T % nq == 0 and tk % L_SEL == 0
    nc = T // L_CMP
    n_sel = T // L_SEL
    kct, vcr = _cmp_layouts(kc, vc)
    q4 = qT.reshape(B, G, HPG * HEAD_DIM, T)
    gates = gT.reshape(B, 3, G, HPG, T).transpose(0, 2, 4, 1, 3).reshape(B, G, T, 3 * HPG)
    va = HEAD_DIM + BF16_SUBLANES
    gb = GROUPS_PER_STEP
    assert G % gb == 0
    once = pl.Buffered(1)
    per_group = lambda rows: pl.BlockSpec((None, gb, rows, T), lambda b, g, i: (b, g, 0, 0), pipeline_mode=once)
    return pl.pallas_call(
        functools.partial(_attn_prompt_kernel, tk=tk, T=T, gb=gb),
        grid=(B, G // gb, T // nq),
        in_specs=[
            pl.BlockSpec((None, gb, HPG * HEAD_DIM, nq), lambda b, g, i: (b, g, 0, i)),
            pl.BlockSpec((None, gb, HEAD_DIM, nc), lambda b, g, i: (b, g, 0, 0)),
            pl.BlockSpec((None, gb, nc, HEAD_DIM), lambda b, g, i: (b, g, 0, 0)),
            per_group(n_sel + HEAD_DIM), per_group(va), per_group(HEAD_DIM), per_group(va),
            pl.BlockSpec((None, gb, nq, 3 * HPG), lambda b, g, i: (b, g, i, 0)),
        ],
        out_specs=pl.BlockSpec((None, nq, gb * HPG * HEAD_DIM), lambda b, g, i: (b, i, g)),
        out_shape=jax.ShapeDtypeStruct((B, T, D_Q), BF16),
        compiler_params=_cparams(("arbitrary", "arbitrary", "arbitrary")),
        name="nsa_attend_prompt",
    )(q4, kct, vcr, ks_aug, vs_aug, kw, vw_aug, gates)


PAGE_LOOP_UNROLL = 8


def _attn_sample_kernel(pt_ref, q_ref, kc_ref, vc_ref, kwc_ref, vwc_ref, kwn_ref, vwn_ref, ksn_ref, vsn_ref,
                        gt_ref, e_ref, dm_ref, ks_hbm, vs_hbm, o_ref,
                        kbuf, vbuf, ksem, vsem, s_s, sel_s, m_s, l_s, acc_s, oc_s, ow_s,
                        *, nchunks, npc, total, past, tq):
    c = pl.program_id(1)
    kslot = _stream_pages(pt_ref, ks_hbm, kbuf, ksem, nchunks=nchunks, npc=npc, total=total)
    vslot = _stream_pages(pt_ref, vs_hbm, vbuf, vsem, nchunks=nchunks, npc=npc, total=total)
    q = q_ref[...]
    nr = q.shape[0]
    ngt = N_KV_HEADS * tq
    pos_q = past + lax.broadcasted_iota(jnp.int32, (nr, 1), 0) % tq
    n_sel = sel_s.shape[1]
    sel_past = past // L_SEL

    def fold(o_full):
        o = o_full * dm_ref[...]
        out = o[:, 0:HEAD_DIM]
        for gi in range(1, N_KV_HEADS):
            out = out + o[:, gi * HEAD_DIM:(gi + 1) * HEAD_DIM]
        return out

    def nt(p, vt):
        return lax.dot_general(p.astype(BF16), vt.astype(BF16), NT_DIMS, preferred_element_type=F32)

    @pl.when(c == 0)
    def _():
        nc = kc_ref.shape[1]
        s_c = jnp.dot(q, kc_ref[...], preferred_element_type=F32)
        cmask = (_cmp_block_of_column(nc) * L_CMP + (L_CMP - 1)) <= pos_q
        p_c = _masked_softmax(s_c, cmask)
        oc_s[...] = fold(nt(p_c, vc_ref[...]))
        imp = p_c[0:ngt]
        for h in range(1, HPG):
            imp = imp + p_c[h * ngt:(h + 1) * ngt]
        imp = imp[:, :n_sel] + imp[:, n_sel:]
        pad = LANES - ngt
        imp = jnp.concatenate([imp, jnp.zeros((pad, n_sel), F32)], axis=0)
        pos_pad = past + lax.broadcasted_iota(jnp.int32, (LANES, 1), 0) % tq
        sel = _select_blocks(imp, pos_pad, n_sel).T[0:ngt]
        sel_s[...] = jnp.concatenate([sel] * HPG, axis=0)

        nwc = kwc_ref.shape[1]
        nwn = kwn_ref.shape[1]
        s_w = jnp.concatenate([jnp.dot(q, kwc_ref[...].astype(BF16), preferred_element_type=F32),
                               jnp.dot(q, kwn_ref[...].astype(BF16), preferred_element_type=F32)], axis=1)
        lane = lax.broadcasted_iota(jnp.int32, (nr, nwc + nwn), 1)
        pos_w = jnp.where(lane < nwc, past - nwc + lane, past + lane - nwc)
        dpos = pos_q - pos_w
        p_w = _masked_softmax(s_w, (dpos >= 0) & (dpos < WINDOW) & (pos_w >= 0))
        ow_s[...] = fold(nt(p_w[:, :nwc], vwc_ref[...]) + nt(p_w[:, nwc:], vwn_ref[...]))

        nsn = ksn_ref.shape[1]
        s_n = jnp.dot(q, ksn_ref[...].astype(BF16), preferred_element_type=F32)
        kpos = past + lax.broadcasted_iota(jnp.int32, (nr, nsn), 1)
        msk = (sel_s[:, sel_past:sel_past + 1] > 0.5) & (kpos <= pos_q)
        s_n = jnp.where(msk, s_n, NEG_INF)
        m0 = jnp.max(s_n, axis=-1, keepdims=True)
        p_n = jnp.where(msk, jnp.exp(s_n - m0), 0.0)
        m_s[...] = m0
        l_s[...] = jnp.sum(p_n, axis=-1, keepdims=True)
        acc_s[...] = nt(p_n, vsn_ref[...])

    def qk(p, carry):
        r0 = pl.multiple_of(p * PAGE_ROWS, PAGE_ROWS)
        c0 = pl.multiple_of(p * PAGE_SIZE, PAGE_SIZE)
        s_s[:, pl.ds(c0, PAGE_SIZE)] = jnp.dot(q, kbuf[kslot, pl.ds(r0, PAGE_ROWS), :].astype(BF16),
                                                preferred_element_type=F32)
        return carry

    lax.fori_loop(0, npc, qk, 0, unroll=PAGE_LOOP_UNROLL)
    nk = npc * PAGE_SIZE
    sel_c = sel_s[:, pl.ds(pl.multiple_of(c * (nk // L_SEL), LANES), nk // L_SEL)]
    chosen = jnp.dot(sel_c.astype(BF16), e_ref[...], preferred_element_type=F32)
    kpos = c * nk + lax.broadcasted_iota(jnp.int32, (nr, nk), 1)
    msk = (chosen > 0.5) & (kpos <= pos_q)
    s = jnp.where(msk, s_s[...], NEG_INF)
    m_old = m_s[...]
    m_new = jnp.maximum(m_old, jnp.max(s, axis=-1, keepdims=True))
    alpha = jnp.exp(m_old - m_new)
    p = jnp.where(msk, jnp.exp(s - m_new), 0.0)
    l_s[...] = alpha * l_s[...] + jnp.sum(p, axis=-1, keepdims=True)
    m_s[...] = m_new
    s_s[...] = p

    def pv(p_, acc):
        r0 = pl.multiple_of(p_ * PAGE_ROWS, PAGE_ROWS)
        c0 = pl.multiple_of(p_ * PAGE_SIZE, PAGE_SIZE)
        return acc + nt(s_s[:, pl.ds(c0, PAGE_SIZE)], vbuf[vslot, pl.ds(r0, PAGE_ROWS), :])

    acc = lax.fori_loop(0, npc, pv, jnp.zeros((nr, PAGE_ROWS), F32), unroll=PAGE_LOOP_UNROLL)
    acc_s[...] = alpha * acc_s[...] + acc

    @pl.when(c == nchunks - 1)
    def _():
        l = l_s[...]
        o_sel = fold(acc_s[...]) / jnp.where(l > 0.0, l, 1.0)
        gt = gt_ref[...]
        o_ref[...] = gt[:, 0:1] * oc_s[...] + gt[:, 1:2] * o_sel + gt[:, 2:3] * ow_s[...]


def _attend_sample(qT, kvT, gT, kc_all, vc_all, win_k, win_v, pool_ks, pool_vs, page_table, *, npc):
    Bs, NPG = page_table.shape
    tq = qT.shape[1] // Bs
    G = N_KV_HEADS
    past = NPG * PAGE_SIZE
    nchunks = NPG // npc
    nk = npc * PAGE_SIZE
    assert NPG % npc == 0 and nk // L_SEL == LANES
    nr = HPG * G * tq
    eye = jnp.eye(G, dtype=qT.dtype)
    q5 = qT.reshape(G, HPG, HEAD_DIM, tq, Bs).transpose(4, 1, 0, 3, 2)
    q_bd = jnp.einsum('bhgtd,gk->bhgtkd', q5, eye).reshape(Bs, nr, G * HEAD_DIM)
    dm = jnp.broadcast_to(jnp.eye(G, dtype=F32)[None, :, None, :, None],
                          (HPG, G, tq, G, HEAD_DIM)).reshape(nr, G * HEAD_DIM)
    gates = gT.reshape(3, G, HPG, tq, Bs).transpose(4, 2, 1, 3, 0).reshape(Bs, nr, 3)

    def new_rows(kind):
        r = kvT[kind].reshape(G * HEAD_DIM, tq, Bs).transpose(2, 0, 1)
        return jnp.pad(r, ((0, 0), (0, 0), (0, LANES - tq)))

    nc = kc_all.shape[1]
    half = nc // CMP_PER_SEL
    n_sel = -(-(half) // LANES) * LANES
    assert past // L_SEL < n_sel

    def cmp_fm(x):
        padw = ((0, 0), (0, 0), (0, n_sel - half))
        parts = [jnp.pad(x[:, par::CMP_PER_SEL].transpose(0, 2, 3, 1).reshape(Bs, G * HEAD_DIM, half), padw)
                 for par in range(CMP_PER_SEL)]
        return jnp.concatenate(parts, axis=2).astype(BF16)

    n_win = win_k.shape[-1]
    wk = win_k.reshape(Bs, G * HEAD_DIM, n_win)
    wv = win_v.reshape(Bs, G * HEAD_DIM, n_win)
    e = _block_expansion(nk // L_SEL, nk)
    ks2 = pool_ks.reshape(pool_ks.shape[0], PAGE_ROWS, PAGE_SIZE)
    vs2 = pool_vs.reshape(pool_vs.shape[0], PAGE_ROWS, PAGE_SIZE)
    per_b = lambda shape: pl.BlockSpec((None,) + shape, lambda b, c, pt: (b, 0, 0))
    kern = functools.partial(_attn_sample_kernel, nchunks=nchunks, npc=npc, total=Bs * nchunks, past=past, tq=tq)
    o = pl.pallas_call(
        kern,
        grid_spec=pltpu.PrefetchScalarGridSpec(
            num_scalar_prefetch=1,
            grid=(Bs, nchunks),
            in_specs=[
                per_b((nr, G * HEAD_DIM)),
                per_b((G * HEAD_DIM, 2 * n_sel)), per_b((G * HEAD_DIM, 2 * n_sel)),
                per_b((G * HEAD_DIM, n_win)), per_b((G * HEAD_DIM, n_win)),
                per_b((G * HEAD_DIM, LANES)), per_b((G * HEAD_DIM, LANES)),
                per_b((G * HEAD_DIM, LANES)), per_b((G * HEAD_DIM, LANES)),
                per_b((nr, 3)),
                pl.BlockSpec(e.shape, lambda b, c, pt: (0, 0)),
                pl.BlockSpec(dm.shape, lambda b, c, pt: (0, 0)),
                pl.BlockSpec(memory_space=pl.ANY),
                pl.BlockSpec(memory_space=pl.ANY),
            ],
            out_specs=pl.BlockSpec((None, nr, HEAD_DIM), lambda b, c, pt: (b, 0, 0)),
            scratch_shapes=[
                pltpu.VMEM((2, npc * PAGE_ROWS, PAGE_SIZE), F32), pltpu.VMEM((2, npc * PAGE_ROWS, PAGE_SIZE), F32),
                pltpu.SemaphoreType.DMA((2,)), pltpu.SemaphoreType.DMA((2,)),
                pltpu.VMEM((nr, nk), F32), pltpu.VMEM((nr, n_sel), F32),
                pltpu.VMEM((nr, 1), F32), pltpu.VMEM((nr, 1), F32), pltpu.VMEM((nr, G * HEAD_DIM), F32),
                pltpu.VMEM((nr, HEAD_DIM), F32), pltpu.VMEM((nr, HEAD_DIM), F32),
            ],
        ),
        out_shape=jax.ShapeDtypeStruct((Bs, nr, HEAD_DIM), F32),
        compiler_params=_cparams(("arbitrary", "arbitrary")),
        name="nsa_attend_sample",
    )(page_table, q_bd, cmp_fm(kc_all), cmp_fm(vc_all), wk, wv, new_rows(4), new_rows(5), new_rows(2), new_rows(3),
      gates, e, dm, ks2, vs2)
    o = o.reshape(Bs, HPG, G, tq, HEAD_DIM).transpose(3, 0, 2, 1, 4)
    return o.reshape(tq * Bs, D_Q).astype(BF16)


def _out_proj_kernel(a_ref, w_ref, x_ref, o_ref):
    o_ref[...] = x_ref[...] + jnp.dot(a_ref[...], w_ref[...], preferred_element_type=F32)


def _out_proj(a, w, x, *, tm):
    R, Kd = a.shape
    N = w.shape[1]
    assert R % tm == 0
    return pl.pallas_call(
        _out_proj_kernel,
        grid=(R // tm,),
        in_specs=[pl.BlockSpec((tm, Kd), lambda i: (i, 0)), pl.BlockSpec((Kd, N), lambda i: (0, 0)),
                  pl.BlockSpec((tm, N), lambda i: (i, 0))],
        out_specs=pl.BlockSpec((tm, N), lambda i: (i, 0)),
        out_shape=jax.ShapeDtypeStruct((R, N), F32),
        compiler_params=_cparams(("arbitrary",)),
        name="out_proj",
    )(a, w.astype(BF16), x)


PROMPT_ROW_TILE = 512
S5_CHUNK = 256
SEL_KEY_TILE = 1024
SAMPLE_PAGES_PER_CHUNK = 64


def _feature_major(cache):
    nd = cache.ndim
    return jnp.moveaxis(cache, nd - 3, nd - 1)


def _nsa_prompt_layer(x, g, w_in, q_gain, k_gain, ck, cv, w_out):
    B, T, D = x.shape
    G = N_KV_HEADS
    outs = _nsa_project(x, jnp.arange(T), g, w_in, q_gain, k_gain, tm=PROMPT_ROW_TILE, paged=True)
    qT, gT = outs[:2]
    kinds = outs[2:2 + N_KINDS]
    ks_aug, vs_aug, kw, vw_aug, kcp, vcp = outs[2 + N_KINDS:]
    npg = T // PAGE_SIZE
    ptab = jnp.arange(B * npg, dtype=jnp.int32).reshape(B, npg)
    kc = _compress(kcp, ptab, _compress_tables(*ck), npc=npg)
    vc = _compress(vcp, ptab, _compress_tables(*cv), npc=npg)
    o = _attend_prompt(qT, ks_aug, vs_aug, kw, vw_aug, gT, kc, vc, tk=SEL_KEY_TILE)
    y = _out_proj(o.reshape(B * T, D_Q), w_out, x.reshape(B * T, D), tm=PROMPT_ROW_TILE).reshape(B, T, D)
    rows = [k.reshape(B, G, HEAD_DIM, T).transpose(0, 3, 1, 2) for k in kinds]
    n_keep = min(WINDOW, T)
    rows[4] = rows[4][:, T - n_keep:]
    rows[5] = rows[5][:, T - n_keep:]
    return y, rows


def _nsa_sample_layer(x_tm, Bs, pools, win_k, win_v, page_table, g, w_in, q_gain, k_gain, ck, cv, w_out):
    _, R, D = x_tm.shape
    G = N_KV_HEADS
    tq = R // Bs
    npg = page_table.shape[1]
    past = npg * PAGE_SIZE
    pos = past + jnp.arange(R) // Bs
    qT, gT, kvT = _nsa_project(x_tm, pos, g, w_in, q_gain, k_gain, tm=R, paged=False)
    qT, kvT, gT = qT[0], kvT[0], gT[0]
    new_pad = -(-tq // L_SEL) * L_SEL
    assert new_pad <= PAGE_SIZE
    new_tab = jnp.arange(Bs, dtype=jnp.int32).reshape(1, Bs)

    def cmp_all(pool, kind, tabs):
        past_blocks = _compress(_feature_major(pool), page_table, tabs, npc=SAMPLE_PAGES_PER_CHUNK)
        page = kvT[kind].reshape(G, HEAD_DIM, tq, Bs).transpose(3, 0, 1, 2)
        page = jnp.pad(page, ((0, 0), (0, 0), (0, 0), (0, PAGE_SIZE - tq)))
        new_blocks = _compress(page, new_tab, tabs, npc=Bs).reshape(Bs, CMP_PER_PAGE, G, HEAD_DIM)
        return jnp.concatenate([past_blocks, new_blocks[:, :new_pad // L_CMP]], axis=1)

    kc_all = cmp_all(pools[0], 0, _compress_tables(*ck))
    vc_all = cmp_all(pools[1], 1, _compress_tables(*cv))
    o = _attend_sample(qT, kvT, gT, kc_all, vc_all, _feature_major(win_k), _feature_major(win_v),
                       _feature_major(pools[2]), _feature_major(pools[3]), page_table,
                       npc=SAMPLE_PAGES_PER_CHUNK)
    y = _out_proj(o, w_out, x_tm[0], tm=R)[None]
    rows = [kvT[k].reshape(G, HEAD_DIM, tq, Bs).transpose(3, 2, 0, 1) for k in range(N_KINDS)]
    return y, rows


def kernel(x_prompt, x_sample, cache_k_cmp, cache_v_cmp, cache_k_slc, cache_v_slc, cache_k_win, cache_v_win,
           state_ssm_re, state_ssm_im, state_conv, page_table, norm_mix, norm_ffn,
           ssm_w_in, ssm_a_re, ssm_a_im, ssm_log_dt, ssm_b_re, ssm_b_im, ssm_c_re, ssm_c_im,
           ssm_d, ssm_w_gate, ssm_w_out,
           nsa_w_in, nsa_q_gain, nsa_k_gain, nsa_ck_w1, nsa_ck_b1, nsa_ck_w2,
           nsa_cv_w1, nsa_cv_b1, nsa_cv_w2, nsa_w_out,
           ffn_w_up, ffn_conv_w, ffn_conv_b, ffn_w_down):
    B, T, D = x_prompt.shape
    Bs, Tq, _ = x_sample.shape
    depth = norm_mix.shape[0]
    Fd = ffn_w_down.shape[1]
    G, P = ssm_a_re.shape[1:]
    xp = x_prompt
    xs = x_sample.transpose(1, 0, 2).reshape(1, Tq * Bs, D)
    p_sre, p_sim, s_sre, s_sim, p_cv, s_cv = [], [], [], [], [], []
    p_rows = [[] for _ in range(N_KINDS)]
    s_rows = [[] for _ in range(N_KINDS)]
    for i in range(depth):
        li = i // 2
        if i % 2 == 0:
            tabs = _s5_tables(ssm_a_re[li], ssm_a_im[li], ssm_log_dt[li], ssm_b_re[li], ssm_b_im[li],
                              ssm_c_re[li], ssm_c_im[li])
            sp = (norm_mix[i], ssm_w_in[li], tabs, ssm_d[li], ssm_w_gate[li], ssm_w_out[li])
            z = jnp.zeros((B, SUBLANES, G * P), F32)
            xp, fr, fi = _s5_layer(xp, z, z, *sp, tc=min(S5_CHUNK, T), slab=1)
            p_sre.append(fr[:, SUBLANES - 1].reshape(B, G, P))
            p_sim.append(fi[:, SUBLANES - 1].reshape(B, G, P))
            xs, fr, fi = _s5_layer(xs, state_ssm_re[li].reshape(1, Bs, G * P).astype(F32),
                                   state_ssm_im[li].reshape(1, Bs, G * P).astype(F32), *sp, tc=Tq * Bs, slab=Bs)
            s_sre.append(fr.reshape(Bs, G, P))
            s_sim.append(fi.reshape(Bs, G, P))
        else:
            ap = (norm_mix[i], nsa_w_in[li], nsa_q_gain[li], nsa_k_gain[li],
                  (nsa_ck_w1[li], nsa_ck_b1[li], nsa_ck_w2[li]), (nsa_cv_w1[li], nsa_cv_b1[li], nsa_cv_w2[li]),
                  nsa_w_out[li])
            xp, rp = _nsa_prompt_layer(xp, *ap)
            xs, rs = _nsa_sample_layer(xs, Bs, (cache_k_cmp[li], cache_v_cmp[li], cache_k_slc[li], cache_v_slc[li]),
                                       cache_k_win[li], cache_v_win[li], page_table, *ap)
            for j in range(N_KINDS):
                p_rows[j].append(rp[j])
                s_rows[j].append(rs[j])
        fw = (norm_ffn[i], ffn_w_up[i], ffn_conv_w[i], ffn_conv_b[i], ffn_w_down[i])
        xp, cp = _conv_ffn(xp, jnp.zeros((B, SUBLANES, Fd), F32), *fw, shift=1, tm=min(PROMPT_ROW_TILE, T))
        p_cv.append(cp[:, SUBLANES - (CONV_W - 1):])
        init = state_conv[i].astype(F32).transpose(1, 0, 2).reshape(1, (CONV_W - 1) * Bs, Fd)
        xs, cs = _conv_ffn(xs, init, *fw, shift=Bs, tm=Tq * Bs)
        s_cv.append(cs.reshape(CONV_W - 1, Bs, Fd).transpose(1, 0, 2))
    st = jnp.stack
    y_sample = xs.reshape(Tq, Bs, D).transpose(1, 0, 2)
    return (xp, y_sample,
            st(p_sre), st(p_sim), *[st(r) for r in p_rows], st(p_cv),
            st(s_sre), st(s_sim), *[st(r) for r in s_rows], st(s_cv))
```
